```python
import math
import jax, jax.numpy as jnp
from jax import lax
import numpy as np

D_MODEL = 2048
BATCH = 4
SEQ = 8192
DEPTH = 4

MIX_WIDTH = D_MODEL
DN_HEADS = 8
DN_HEAD_DIM = 128
DN_WIDTH = DN_HEADS * DN_HEAD_DIM
DN_CHUNK = 64
SHORT_CONV = 4
POOL_WINDOWS = (2, 4, 8, 16)
POOL_GROUPS = len(POOL_WINDOWS)
POOL_WIDTH = MIX_WIDTH - DN_WIDTH
POOL_GROUP_DIM = POOL_WIDTH // POOL_GROUPS
EVEN_IN = 4 * DN_WIDTH + 2 * DN_HEADS + POOL_WIDTH
CONF_WIDTH = D_MODEL
CONF_WIN = 31
D_FF = 4 * D_MODEL
N_EVEN = (DEPTH + 1) // 2
N_ODD = DEPTH // 2
EPS = 1e-6

kernel_name = 'hybrid_deltanet_pool_conformer_trunk'


def rms_norm(x, g):
    xf = x.astype(jnp.float32)
    y = xf * lax.rsqrt(jnp.mean(xf * xf, axis=-1, keepdims=True) + EPS)
    return (y * g.astype(jnp.float32)).astype(x.dtype)


def layer_norm(x, g, b):
    xf = x.astype(jnp.float32)
    mu = jnp.mean(xf, axis=-1, keepdims=True)
    xc = xf - mu
    y = xc * lax.rsqrt(jnp.mean(xc * xc, axis=-1, keepdims=True) + EPS)
    return (y * g.astype(jnp.float32) + b.astype(jnp.float32)).astype(x.dtype)


def l2_normalize(x):
    return x * lax.rsqrt(jnp.sum(x * x, axis=-1, keepdims=True) + EPS)


def causal_depthwise_conv(x, w):
    K, C = w.shape
    return lax.conv_general_dilated(
        x, w[:, None, :].astype(x.dtype), window_strides=(1,), padding=[(K - 1, 0)],
        dimension_numbers=('NWC', 'WIO', 'NWC'), feature_group_count=C)


def gated_delta_rule(q, k, v, g, beta):
    B_, S_, H, Dk = q.shape
    Dv = v.shape[-1]
    C = DN_CHUNK
    N = S_ // C

    def to_chunks(t):
        t = t.reshape((B_, N, C, H) + t.shape[3:])
        return jnp.moveaxis(t, 3, 1)

    q, k, v, g, beta = map(to_chunks, (q * Dk ** -0.5, k, v, g, beta))
    gc = jnp.cumsum(g, axis=-1)
    pos = jnp.arange(C)
    causal = pos[:, None] >= pos[None, :]
    strict = pos[:, None] > pos[None, :]
    gamma = jnp.exp(jnp.where(causal, gc[..., :, None] - gc[..., None, :], -jnp.inf))
    kb = k * beta[..., None]
    a_mat = jnp.where(strict, jnp.einsum('bhnck,bhnmk->bhncm', kb, k) * gamma, 0.0)
    eye = jnp.eye(C, dtype=jnp.float32)
    t_inv = lax.linalg.triangular_solve(a_mat + eye, jnp.broadcast_to(eye, a_mat.shape),
                                        left_side=True, lower=True, unit_diagonal=True)
    u = jnp.einsum('bhncm,bhnmv->bhncv', t_inv, v * beta[..., None])
    w = jnp.einsum('bhncm,bhnmk->bhnck', t_inv, kb * jnp.exp(gc)[..., None])
    attn = jnp.where(causal, jnp.einsum('bhnck,bhnmk->bhncm', q, k) * gamma, 0.0)
    qg = q * jnp.exp(gc)[..., None]
    g_last = gc[..., -1]
    kd = k * jnp.exp(g_last[..., None] - gc)[..., None]
    decay = jnp.exp(g_last)

    def step(state, inp):
        u_n, w_n, attn_n, qg_n, kd_n, decay_n = inp
        v_new = u_n - jnp.einsum('bhck,bhkv->bhcv', w_n, state)
        o_n = (jnp.einsum('bhck,bhkv->bhcv', qg_n, state)
               + jnp.einsum('bhcm,bhmv->bhcv', attn_n, v_new))
        state = state * decay_n[..., None, None] + jnp.einsum('bhck,bhcv->bhkv', kd_n, v_new)
        return state, o_n

    xs = tuple(jnp.moveaxis(t, 2, 0) for t in (u, w, attn, qg, kd, decay))
    state0 = jnp.zeros((B_, H, Dk, Dv), jnp.float32)
    _, o = lax.scan(step, state0, xs)
    return jnp.transpose(o, (1, 0, 3, 2, 4)).reshape(B_, S_, H, Dv)


def multiscale_pool(x, w_grp, scale):
    B_, S_, _ = x.shape
    xf = x.astype(jnp.float32)
    cs = jnp.cumsum(xf, axis=1)
    count = jnp.arange(1, S_ + 1, dtype=jnp.float32)[:, None]
    outs = []
    for gi, win in enumerate(POOL_WINDOWS):
        sl = slice(gi * POOL_GROUP_DIM, (gi + 1) * POOL_GROUP_DIM)
        c = cs[..., sl]
        lower = jnp.pad(c, ((0, 0), (win, 0), (0, 0)))[:, :S_]
        outs.append((c - lower) / jnp.minimum(count, win) - xf[..., sl])
    pooled = jnp.concatenate(outs, axis=-1).astype(x.dtype).reshape(B_, S_, POOL_GROUPS, POOL_GROUP_DIM)
    y = jnp.einsum('bsgc,gcd->bsgd', pooled, w_grp).reshape(B_, S_, POOL_WIDTH)
    return y * scale


def even_mixer(h, w_in, conv_w, a_log, dt_bias, dn_norm, pool_w, pool_scale, w_out):
    B_, S_, _ = h.shape
    p = h @ w_in
    o1 = 3 * DN_WIDTH
    o2 = o1 + DN_WIDTH
    o3 = o2 + DN_HEADS
    o4 = o3 + DN_HEADS
    qkv = jax.nn.silu(causal_depthwise_conv(p[..., :o1], conv_w))
    z = p[..., o1:o2].reshape(B_, S_, DN_HEADS, DN_HEAD_DIM)
    b_logit = p[..., o2:o3].astype(jnp.float32)
    a_logit = p[..., o3:o4].astype(jnp.float32)
    xp = p[..., o4:]
    q, k, v = [t.reshape(B_, S_, DN_HEADS, DN_HEAD_DIM).astype(jnp.float32)
               for t in jnp.split(qkv, 3, axis=-1)]
    q = l2_normalize(q)
    k = l2_normalize(k)
    beta = jax.nn.sigmoid(b_logit)
    g = -jnp.exp(a_log.astype(jnp.float32)) * jax.nn.softplus(a_logit + dt_bias.astype(jnp.float32))
    o = gated_delta_rule(q, k, v, g, beta).astype(h.dtype)
    o = (rms_norm(o, dn_norm) * jax.nn.silu(z)).reshape(B_, S_, DN_WIDTH)
    y_pool = multiscale_pool(xp, pool_w, pool_scale)
    return jnp.concatenate([o, y_pool], axis=-1) @ w_out


def odd_mixer(h, w_in, dw_w, dw_b, ln_g, ln_b, w_out):
    a, gate = jnp.split(h @ w_in, 2, axis=-1)
    u = a * jax.nn.sigmoid(gate)
    u = causal_depthwise_conv(u, dw_w) + dw_b
    u = jax.nn.silu(layer_norm(u, ln_g, ln_b))
    return u @ w_out


def squared_relu_mlp(h, w_up, w_down):
    return jnp.square(jax.nn.relu(h @ w_up)) @ w_down


def setup_inputs(seed: int = 0) -> dict:
    key = jax.random.key(seed)
    ks = jax.random.split(key, 24)
    f32 = jnp.float32

    def nrm(k, shape, fan_in):
        return jax.random.normal(k, shape, f32) * fan_in ** -0.5

    def gain(k, shape):
        return 1.0 + 0.05 * jax.random.normal(k, shape, f32)

    dt = jnp.exp(jax.random.uniform(ks[9], (N_EVEN, DN_HEADS), f32, math.log(1e-3), math.log(1e-1)))
    return {
        'x': jax.random.normal(ks[0], (BATCH, SEQ, D_MODEL), f32),
        'norm_mix_pre': gain(ks[1], (DEPTH, D_MODEL)),
        'norm_mix_post': gain(ks[2], (DEPTH, D_MODEL)),
        'norm_mlp_pre': gain(ks[3], (DEPTH, D_MODEL)),
        'norm_mlp_post': gain(ks[4], (DEPTH, D_MODEL)),
        'even_w_in': nrm(ks[5], (N_EVEN, D_MODEL, EVEN_IN), D_MODEL),
        'even_conv': nrm(ks[6], (N_EVEN, SHORT_CONV, 3 * DN_WIDTH), SHORT_CONV),
        'even_a_log': jnp.log(jax.random.uniform(ks[8], (N_EVEN, DN_HEADS), f32, 1.0, 16.0)),
        'even_dt_bias': dt + jnp.log(-jnp.expm1(-dt)),
        'even_dn_norm': gain(ks[10], (N_EVEN, DN_HEAD_DIM)),
        'even_pool_w': nrm(ks[11], (N_EVEN, POOL_GROUPS, POOL_GROUP_DIM, POOL_GROUP_DIM), POOL_GROUP_DIM),
        'even_pool_scale': gain(ks[12], (N_EVEN, POOL_WIDTH)),
        'even_w_out': nrm(ks[13], (N_EVEN, MIX_WIDTH, D_MODEL), MIX_WIDTH),
        'odd_w_in': nrm(ks[14], (N_ODD, D_MODEL, 2 * CONF_WIDTH), D_MODEL),
        'odd_dw': nrm(ks[15], (N_ODD, CONF_WIN, CONF_WIDTH), CONF_WIN),
        'odd_dw_b': 0.02 * jax.random.normal(ks[16], (N_ODD, CONF_WIDTH), f32),
        'odd_ln_g': gain(ks[17], (N_ODD, CONF_WIDTH)),
        'odd_ln_b': 0.02 * jax.random.normal(ks[18], (N_ODD, CONF_WIDTH), f32),
        'odd_w_out': nrm(ks[19], (N_ODD, CONF_WIDTH, D_MODEL), CONF_WIDTH),
        'mlp_w_up': nrm(ks[20], (DEPTH, D_MODEL, D_FF), D_MODEL),
        'mlp_w_down': nrm(ks[21], (DEPTH, D_FF, D_MODEL), D_FF),
    }


def reference(x, norm_mix_pre, norm_mix_post, norm_mlp_pre, norm_mlp_post,
              even_w_in, even_conv, even_a_log, even_dt_bias, even_dn_norm,
              even_pool_w, even_pool_scale, even_w_out,
              odd_w_in, odd_dw, odd_dw_b, odd_ln_g, odd_ln_b, odd_w_out,
              mlp_w_up, mlp_w_down):
    for i in range(DEPTH):
        j = i // 2
        h = rms_norm(x, norm_mix_pre[i])
        if i % 2 == 0:
            mix = even_mixer(h, even_w_in[j], even_conv[j], even_a_log[j], even_dt_bias[j],
                             even_dn_norm[j], even_pool_w[j], even_pool_scale[j], even_w_out[j])
        else:
            mix = odd_mixer(h, odd_w_in[j], odd_dw[j], odd_dw_b[j], odd_ln_g[j], odd_ln_b[j], odd_w_out[j])
        x = x + rms_norm(mix, norm_mix_post[i])
        ff = squared_relu_mlp(rms_norm(x, norm_mlp_pre[i]), mlp_w_up[i], mlp_w_down[i])
        x = x + rms_norm(ff, norm_mlp_post[i])
    return x
```

```python
import functools
import math

import jax
import jax.numpy as jnp
from jax import lax
from jax.experimental import pallas as pl
from jax.experimental.pallas import tpu as pltpu

F32 = jnp.float32
BF16 = jnp.bfloat16
EPS = 1e-6
LANES = 128
DN_CHUNK = 64
SHORT_CONV = 4
POOL_WINDOWS = (2, 4, 8, 16)
VMEM_LIMIT = 56 * 1024 * 1024
HIGHEST = lax.Precision.HIGHEST


def _cparams(*sem):
    return pltpu.CompilerParams(dimension_semantics=sem, vmem_limit_bytes=VMEM_LIMIT)


def _dot(a, b, precision=None):
    return jnp.dot(a, b, preferred_element_type=F32, precision=precision)


def _dot_nt(a, b, precision=None):
    return lax.dot_general(a, b, (((1,), (1,)), ((), ())), preferred_element_type=F32, precision=precision)


def _dot_tn(a, b):
    return lax.dot_general(a, b, (((0,), (0,)), ((), ())), preferred_element_type=F32)


def _rms(x, g):
    ms = jnp.mean(x * x, axis=-1, keepdims=True)
    return x * lax.rsqrt(ms + EPS) * g


def _silu(x):
    return x * jax.nn.sigmoid(x)


def _norm_proj_kernel(x_ref, g_ref, w_ref, o_ref, xn_ref):
    @pl.when(pl.program_id(1) == 0)
    def _():
        xn_ref[...] = _rms(x_ref[...], g_ref[...]).astype(BF16)

    o_ref[...] = _dot(xn_ref[...], w_ref[...])


def _norm_glu_kernel(x_ref, g_ref, wa_ref, wg_ref, o_ref, xn_ref):
    @pl.when(pl.program_id(1) == 0)
    def _():
        xn_ref[...] = _rms(x_ref[...], g_ref[...]).astype(BF16)

    xn = xn_ref[...]
    a = _dot(xn, wa_ref[...])
    gate = _dot(xn, wg_ref[...])
    o_ref[...] = a * jax.nn.sigmoid(gate)


def norm_proj(x, g, w, *, tm, tn):
    m, d = x.shape
    n = w.shape[1]
    return pl.pallas_call(
        _norm_proj_kernel,
        out_shape=jax.ShapeDtypeStruct((m, n), F32),
        grid=(m // tm, n // tn),
        in_specs=[pl.BlockSpec((tm, d), lambda i, j: (i, 0)),
                  pl.BlockSpec((1, d), lambda i, j: (0, 0)),
                  pl.BlockSpec((d, tn), lambda i, j: (0, j))],
        out_specs=pl.BlockSpec((tm, tn), lambda i, j: (i, j)),
        scratch_shapes=[pltpu.VMEM((tm, d), BF16)],
        compiler_params=_cparams("parallel", "arbitrary"),
        name="norm_proj",
    )(x, g, w)


def norm_glu(x, g, w, *, tm, tn):
    m, d = x.shape
    n = w.shape[1] // 2
    nj = n // tn
    return pl.pallas_call(
        _norm_glu_kernel,
        out_shape=jax.ShapeDtypeStruct((m, n), F32),
        grid=(m // tm, nj),
        in_specs=[pl.BlockSpec((tm, d), lambda i, j: (i, 0)),
                  pl.BlockSpec((1, d), lambda i, j: (0, 0)),
                  pl.BlockSpec((d, tn), lambda i, j: (0, j)),
                  pl.BlockSpec((d, tn), lambda i, j: (0, j + nj))],
        out_specs=pl.BlockSpec((tm, tn), lambda i, j: (i, j)),
        scratch_shapes=[pltpu.VMEM((tm, d), BF16)],
        compiler_params=_cparams("parallel", "arbitrary"),
        name="norm_glu",
    )(x, g, w, w)


def _out_proj_kernel(*refs, n_in):
    a_refs = refs[:n_in]
    w_refs = refs[n_in:2 * n_in]
    x_ref, g_ref, o_ref = refs[2 * n_in:]
    mix = _dot(a_refs[0][...], w_refs[0][...])
    for a_ref, w_ref in zip(a_refs[1:], w_refs[1:]):
        mix = mix + _dot(a_ref[...], w_ref[...])
    o_ref[...] = x_ref[...] + _rms(mix, g_ref[...])


def out_proj(acts, ws, x, g, *, tm):
    m, d = x.shape
    n_in = len(acts)
    in_specs = ([pl.BlockSpec((tm, a.shape[1]), lambda i: (i, 0)) for a in acts]
                + [pl.BlockSpec(w.shape, lambda i: (0, 0)) for w in ws]
                + [pl.BlockSpec((tm, d), lambda i: (i, 0)), pl.BlockSpec((1, d), lambda i: (0, 0))])
    return pl.pallas_call(
        functools.partial(_out_proj_kernel, n_in=n_in),
        out_shape=jax.ShapeDtypeStruct((m, d), F32),
        grid=(m // tm,),
        in_specs=in_specs,
        out_specs=pl.BlockSpec((tm, d), lambda i: (i, 0)),
        compiler_params=_cparams("parallel"),
        name="out_proj",
    )(*acts, *ws, x, g)


def _mlp_kernel(x_ref, gpre_ref, wup_ref, wdn_ref, gpost_ref, o_ref, xn_ref, acc_ref):
    f = pl.program_id(1)

    @pl.when(f == 0)
    def _():
        xn_ref[...] = _rms(x_ref[...], gpre_ref[...]).astype(BF16)
        acc_ref[...] = jnp.zeros_like(acc_ref)

    h = _dot(xn_ref[...], wup_ref[...])
    h = jnp.square(jnp.maximum(h, 0.0)).astype(BF16)
    acc_ref[...] += _dot(h, wdn_ref[...])

    @pl.when(f == pl.num_programs(1) - 1)
    def _():
        o_ref[...] = x_ref[...] + _rms(acc_ref[...], gpost_ref[...])


def mlp(x, gpre, wup, wdn, gpost, *, tm, tf):
    m, d = x.shape
    ff = wup.shape[1]
    return pl.pallas_call(
        _mlp_kernel,
        out_shape=jax.ShapeDtypeStruct((m, d), F32),
        grid=(m // tm, ff // tf),
        in_specs=[pl.BlockSpec((tm, d), lambda i, f: (i, 0)),
                  pl.BlockSpec((1, d), lambda i, f: (0, 0)),
                  pl.BlockSpec((d, tf), lambda i, f: (0, f)),
                  pl.BlockSpec((tf, d), lambda i, f: (f, 0)),
                  pl.BlockSpec((1, d), lambda i, f: (0, 0))],
        out_specs=pl.BlockSpec((tm, d), lambda i, f: (i, 0)),
        scratch_shapes=[pltpu.VMEM((tm, d), BF16), pltpu.VMEM((tm, d), F32)],
        compiler_params=_cparams("parallel", "arbitrary"),
        name="mlp",
    )(x, gpre, wup, wdn, gpost)


def _qkv_prep_kernel(p_ref, cw_ref, q_ref, k_ref, v_ref, buf_ref, *, t, n_heads, halo):
    width = n_heads * LANES

    @pl.when(pl.program_id(1) == 0)
    def _():
        buf_ref[0:halo, :] = jnp.zeros((halo, 3 * width), F32)

    buf_ref[halo:halo + t, :] = p_ref[0]
    outs = (q_ref, k_ref, v_ref)
    for cg in range(3 * n_heads):
        cols = slice(cg * LANES, (cg + 1) * LANES)
        acc = None
        for j in range(SHORT_CONV):
            r0 = halo - (SHORT_CONV - 1) + j
            term = cw_ref[j:j + 1, cols] * buf_ref[r0:r0 + t, cols]
            acc = term if acc is None else acc + term
        y = _silu(acc)
        which, hh = divmod(cg, n_heads)
        if which < 2:
            y = y * lax.rsqrt(jnp.sum(y * y, axis=-1, keepdims=True) + EPS)
        if which == 0:
            y = y * (LANES ** -0.5)
        outs[which][0, :, hh * LANES:(hh + 1) * LANES] = y
    buf_ref[0:halo, :] = buf_ref[t:t + halo, :]


def qkv_prep(p3, conv_w, *, n_heads, t):
    b, s, _ = p3.shape
    width = n_heads * LANES
    halo = 8
    out = jax.ShapeDtypeStruct((b, s, width), F32)
    ospec = pl.BlockSpec((1, t, width), lambda bi, ti: (bi, ti, 0))
    return pl.pallas_call(
        functools.partial(_qkv_prep_kernel, t=t, n_heads=n_heads, halo=halo),
        out_shape=(out, out, out),
        grid=(b, s // t),
        in_specs=[pl.BlockSpec((1, t, 3 * width), lambda bi, ti: (bi, ti, 0)),
                  pl.BlockSpec((SHORT_CONV, 3 * width), lambda bi, ti: (0, 0))],
        out_specs=(ospec, ospec, ospec),
        scratch_shapes=[pltpu.VMEM((t + halo, 3 * width), F32)],
        compiler_params=_cparams("parallel", "arbitrary"),
        name="qkv_prep",
    )(p3, conv_w)


def _tri_inv(a_mat, eye):
    c = a_mat.shape[0]
    x = -a_mat
    p = eye + x
    for _ in range(int(math.log2(c)) - 1):
        x = _dot(x, x, HIGHEST)
        p = p + _dot(p, x, HIGHEST)
    return p


def _delta_kernel(q_ref, k_ref, v_ref, lg_ref, z_ref, alog_ref, dt_ref, dn_ref, o_ref,
                  state_ref, gc_ref, beta_ref, u_ref, w_ref, qg_ref, kd_ref, attn_ref, dec_ref, os_ref,
                  *, t, n_heads):
    c = DN_CHUNK
    nc = t // c

    @pl.when(pl.program_id(1) == 0)
    def _():
        state_ref[...] = jnp.zeros_like(state_ref)

    lg = lg_ref[0]
    beta_all = jax.nn.sigmoid(lg)
    sp_in = lg + dt_ref[...]
    softplus = jnp.maximum(sp_in, 0.0) + jnp.log1p(jnp.exp(-jnp.abs(sp_in)))
    g_all = -jnp.exp(alog_ref[...]) * softplus
    row = lax.broadcasted_iota(jnp.int32, (t, t), 0)
    col = lax.broadcasted_iota(jnp.int32, (t, t), 1)
    tri = ((row // c == col // c) & (col <= row)).astype(F32)
    gc_all = _dot(tri, g_all, HIGHEST)
    for h in range(n_heads):
        gc_ref[h] = jnp.broadcast_to(gc_all[:, n_heads + h:n_heads + h + 1], (t, LANES))
        beta_ref[h] = jnp.broadcast_to(beta_all[:, h:h + 1], (t, LANES))

    ri = lax.broadcasted_iota(jnp.int32, (c, c), 0)
    ci = lax.broadcasted_iota(jnp.int32, (c, c), 1)
    causal = ri >= ci
    strict = ri > ci
    eye = (ri == ci).astype(F32)
    pick0 = (lax.broadcasted_iota(jnp.int32, (c, LANES), 1) == 0).astype(F32)

    def intra(ic, carry):
        r0 = pl.multiple_of(ic * c, c)
        rows = pl.ds(r0, c)
        for h in range(n_heads):
            cols = slice(h * LANES, (h + 1) * LANES)
            qh = q_ref[0, rows, cols]
            kh = k_ref[0, rows, cols]
            vh = v_ref[0, rows, cols]
            gcb = gc_ref[h, rows, :]
            bb = beta_ref[h, rows, :]
            gcr = _dot_nt(pick0, gcb, HIGHEST)
            br = _dot_nt(pick0, bb, HIGHEST)
            gamma = jnp.exp(jnp.where(causal, gcb[:, :c] - gcr, -1e30))
            k16 = kh.astype(BF16)
            q16 = qh.astype(BF16)
            kk = _dot_nt(k16, k16)
            qk = _dot_nt(q16, k16)
            a_mat = jnp.where(strict, bb[:, :c] * kk * gamma, 0.0)
            tb = _tri_inv(a_mat, eye) * br
            u = _dot(tb.astype(BF16), vh.astype(BF16))
            w = _dot((tb * jnp.exp(gcr)).astype(BF16), k16)
            g_last = gcb[c - 1:c, :]
            u_ref[rows, cols] = u
            w_ref[rows, cols] = w.astype(BF16)
            qg_ref[rows, cols] = (qh * jnp.exp(gcb)).astype(BF16)
            kd_ref[rows, cols] = (kh * jnp.exp(g_last - gcb)).astype(BF16)
            attn_ref[h, rows, :] = qk * gamma
            dec_ref[ic, h:h + 1, :] = jnp.exp(g_last)
        return carry

    lax.fori_loop(0, nc, intra, 0)

    def inter(ic, carry):
        r0 = pl.multiple_of(ic * c, c)
        rows = pl.ds(r0, c)
        for h in range(n_heads):
            cols = slice(h * LANES, (h + 1) * LANES)
            st = state_ref[h]
            st16 = st.astype(BF16)
            v_new = u_ref[rows, cols] - _dot(w_ref[rows, cols], st16)
            vn16 = v_new.astype(BF16)
            os_ref[rows, cols] = (_dot(qg_ref[rows, cols], st16)
                                  + _dot(attn_ref[h, rows, :].astype(BF16), vn16))
            state_ref[h] = st * dec_ref[ic, h:h + 1, :] + _dot_tn(kd_ref[rows, cols], vn16)
        return carry

    lax.fori_loop(0, nc, inter, 0)

    for h in range(n_heads):
        cols = slice(h * LANES, (h + 1) * LANES)
        o_ref[0, :, cols] = (_rms(os_ref[:, cols], dn_ref[...]) * _silu(z_ref[0, :, cols])).astype(BF16)


def delta_rule(q, k, v, p3, alog_pad, dt_pad, dn_norm, *, n_heads, t, z_blk, lg_blk):
    b, s, width = q.shape
    c = DN_CHUNK
    qspec = pl.BlockSpec((1, t, width), lambda bi, ti: (bi, ti, 0))
    row = pl.BlockSpec((1, LANES), lambda bi, ti: (0, 0))
    return pl.pallas_call(
        functools.partial(_delta_kernel, t=t, n_heads=n_heads),
        out_shape=jax.ShapeDtypeStruct((b, s, width), BF16),
        grid=(b, s // t),
        in_specs=[qspec, qspec, qspec,
                  pl.BlockSpec((1, t, LANES), lambda bi, ti: (bi, ti, lg_blk)),
                  pl.BlockSpec((1, t, width), lambda bi, ti: (bi, ti, z_blk)),
                  row, row, row],
        out_specs=qspec,
        scratch_shapes=[pltpu.VMEM((n_heads, LANES, LANES), F32),
                        pltpu.VMEM((n_heads, t, LANES), F32),
                        pltpu.VMEM((n_heads, t, LANES), F32),
                        pltpu.VMEM((t, width), F32),
                        pltpu.VMEM((t, width), BF16),
                        pltpu.VMEM((t, width), BF16),
                        pltpu.VMEM((t, width), BF16),
                        pltpu.VMEM((n_heads, t, c), F32),
                        pltpu.VMEM((t // c, n_heads, LANES), F32),
                        pltpu.VMEM((t, width), F32)],
        compiler_params=_cparams("parallel", "arbitrary"),
        name="delta_rule",
    )(q, k, v, p3, p3, alog_pad, dt_pad, dn_norm)


def _pool_kernel(xp_ref, w_ref, sc_ref, o_ref, buf_ref, *, t, halo, gdim):
    ti = pl.program_id(1)

    @pl.when(ti == 0)
    def _():
        buf_ref[0:halo, :] = jnp.zeros((halo, buf_ref.shape[1]), F32)

    buf_ref[halo:halo + t, :] = xp_ref[0]
    pos = ti * t + lax.broadcasted_iota(jnp.int32, (t, 1), 0) + 1
    for gi, win in enumerate(POOL_WINDOWS):
        cols = slice(gi * gdim, (gi + 1) * gdim)
        acc = buf_ref[halo:halo + t, cols]
        tok = acc
        for d in range(1, win):
            acc = acc + buf_ref[halo - d:halo - d + t, cols]
        pooled = acc / jnp.minimum(pos, win).astype(F32) - tok
        y = _dot(pooled.astype(BF16), w_ref[gi])
        o_ref[0, :, cols] = (y * sc_ref[:, cols]).astype(BF16)
    buf_ref[0:halo, :] = buf_ref[t:t + halo, :]


def pool_mixer(p3, w_grp, scale, *, t, xp_blk):
    b, s, _ = p3.shape
    groups, gdim, _ = w_grp.shape
    width = groups * gdim
    halo = 16
    return pl.pallas_call(
        functools.partial(_pool_kernel, t=t, halo=halo, gdim=gdim),
        out_shape=jax.ShapeDtypeStruct((b, s, width), BF16),
        grid=(b, s // t),
        in_specs=[pl.BlockSpec((1, t, width), lambda bi, ti: (bi, ti, xp_blk)),
                  pl.BlockSpec((groups, gdim, gdim), lambda bi, ti: (0, 0, 0)),
                  pl.BlockSpec((1, width), lambda bi, ti: (0, 0))],
        out_specs=pl.BlockSpec((1, t, width), lambda bi, ti: (bi, ti, 0)),
        scratch_shapes=[pltpu.VMEM((t + halo, width), F32)],
        compiler_params=_cparams("parallel", "arbitrary"),
        name="pool_mixer",
    )(p3, w_grp, scale)


def _conf_conv_kernel(u_ref, w_ref, b_ref, g_ref, beta_ref, o_ref, buf_ref, y_ref, *, t, halo, kw, rt):
    width = u_ref.shape[2]

    @pl.when(pl.program_id(1) == 0)
    def _():
        buf_ref[0:halo, :] = jnp.zeros((halo, width), F32)

    buf_ref[halo:halo + t, :] = u_ref[0]
    for r in range(t // rt):
        for cg in range(width // LANES):
            cols = slice(cg * LANES, (cg + 1) * LANES)
            acc = jnp.broadcast_to(b_ref[:, cols], (rt, LANES))
            for j in range(kw):
                r0 = halo - (kw - 1) + j + r * rt
                acc = acc + w_ref[j:j + 1, cols] * buf_ref[r0:r0 + rt, cols]
            y_ref[r * rt:(r + 1) * rt, cols] = acc
    y = y_ref[...]
    mu = jnp.mean(y, axis=-1, keepdims=True)
    yc = y - mu
    var = jnp.mean(yc * yc, axis=-1, keepdims=True)
    o_ref[0] = _silu(yc * lax.rsqrt(var + EPS) * g_ref[...] + beta_ref[...]).astype(BF16)
    buf_ref[0:halo, :] = buf_ref[t:t + halo, :]


def conf_conv(u3, dw, dw_b, ln_g, ln_b, *, t):
    b, s, width = u3.shape
    kw = dw.shape[0]
    halo = 32
    row = pl.BlockSpec((1, width), lambda bi, ti: (0, 0))
    blk = pl.BlockSpec((1, t, width), lambda bi, ti: (bi, ti, 0))
    return pl.pallas_call(
        functools.partial(_conf_conv_kernel, t=t, halo=halo, kw=kw, rt=64),
        out_shape=jax.ShapeDtypeStruct((b, s, width), BF16),
        grid=(b, s // t),
        in_specs=[blk, pl.BlockSpec((kw, width), lambda bi, ti: (0, 0)), row, row, row],
        out_specs=blk,
        scratch_shapes=[pltpu.VMEM((t + halo, width), F32), pltpu.VMEM((t, width), F32)],
        compiler_params=_cparams("parallel", "arbitrary"),
        name="conf_conv",
    )(u3, dw, dw_b, ln_g, ln_b)


def _even_layer(x2, b, s, g_pre, g_post, w_in, conv_w, a_log, dt_bias, dn_norm, pool_w, pool_scale, w_out):
    d = x2.shape[1]
    n_heads = a_log.shape[0]
    dn_w = n_heads * dn_norm.shape[0]
    pool_width = pool_scale.shape[0]
    o2 = 4 * dn_w
    o4 = o2 + 2 * n_heads
    n_pad = 768 * pl.cdiv(4 * dn_w + pool_width + LANES, 768)
    w_cat = jnp.concatenate(
        [w_in[:, :o2], w_in[:, o4:], w_in[:, o2:o4],
         jnp.zeros((d, n_pad - w_in.shape[1]), w_in.dtype)], axis=1).astype(BF16)
    p = norm_proj(x2, g_pre.reshape(1, d), w_cat, tm=1024, tn=768)
    p3 = p.reshape(b, s, n_pad)
    q, k, v = qkv_prep(p3, conv_w, n_heads=n_heads, t=512)
    pad = jnp.zeros((n_heads,), F32)
    tail = jnp.zeros((LANES - 2 * n_heads,), F32)
    alog_pad = jnp.concatenate([pad, a_log, tail]).reshape(1, LANES)
    dt_pad = jnp.concatenate([pad, dt_bias, tail]).reshape(1, LANES)
    o = delta_rule(q, k, v, p3, alog_pad, dt_pad, dn_norm.reshape(1, LANES), n_heads=n_heads, t=256,
                   z_blk=3, lg_blk=(4 * dn_w + pool_width) // LANES)
    y_pool = pool_mixer(p3, pool_w.astype(BF16), pool_scale.reshape(1, pool_width), t=512,
                        xp_blk=4 * dn_w // pool_width)
    w_out16 = w_out.astype(BF16)
    return out_proj([o.reshape(b * s, dn_w), y_pool.reshape(b * s, pool_width)],
                    [w_out16[:dn_w], w_out16[dn_w:]], x2, g_post.reshape(1, d), tm=512)


def _odd_layer(x2, b, s, g_pre, g_post, w_in, dw, dw_b, ln_g, ln_b, w_out):
    d = x2.shape[1]
    width = w_out.shape[0]
    u = norm_glu(x2, g_pre.reshape(1, d), w_in.astype(BF16), tm=1024, tn=512)
    a = conf_conv(u.reshape(b, s, width), dw, dw_b.reshape(1, width), ln_g.reshape(1, width),
                  ln_b.reshape(1, width), t=256)
    return out_proj([a.reshape(b * s, width)], [w_out.astype(BF16)], x2, g_post.reshape(1, d), tm=512)


def kernel(x, norm_mix_pre, norm_mix_post, norm_mlp_pre, norm_mlp_post, even_w_in, even_conv, even_a_log, even_dt_bias, even_dn_norm, even_pool_w, even_pool_scale, even_w_out, odd_w_in, odd_dw, odd_dw_b, odd_ln_g, odd_ln_b, odd_w_out, mlp_w_up, mlp_w_down):
    b, s, d = x.shape
    depth = norm_mix_pre.shape[0]
    x2 = x.reshape(b * s, d)
    for i in range(depth):
        j = i // 2
        if i % 2 == 0:
            x2 = _even_layer(x2, b, s, norm_mix_pre[i], norm_mix_post[i], even_w_in[j], even_conv[j],
                             even_a_log[j], even_dt_bias[j], even_dn_norm[j], even_pool_w[j],
                             even_pool_scale[j], even_w_out[j])
        else:
            x2 = _odd_layer(x2, b, s, norm_mix_pre[i], norm_mix_post[i], odd_w_in[j], odd_dw[j],
                            odd_dw_b[j], odd_ln_g[j], odd_ln_b[j], odd_w_out[j])
        x2 = mlp(x2, norm_mlp_pre[i].reshape(1, d), mlp_w_up[i].astype(BF16), mlp_w_down[i].astype(BF16),
                 norm_mlp_post[i].reshape(1, d), tm=512, tf=512)
    return x2.reshape(b, s, d)
```

```python
import functools

import jax
import jax.numpy as jnp
from jax import lax
from jax.experimental import pallas as pl
from jax.experimental.pallas import tpu as pltpu

F32 = jnp.float32
BF16 = jnp.bfloat16
EPS = 1e-6
LANES = 128
DN_CHUNK = 128
SHORT_CONV = 4
POOL_WINDOWS = (2, 4, 8, 16)
VMEM_LIMIT = 56 * 1024 * 1024
HIGHEST = lax.Precision.HIGHEST


def _cparams(*sem):
    return pltpu.CompilerParams(dimension_semantics=sem, vmem_limit_bytes=VMEM_LIMIT)


def _dot(a, b, precision=None):
    return jnp.dot(a, b, preferred_element_type=F32, precision=precision)


def _dot_nt(a, b, precision=None):
    return lax.dot_general(a, b, (((1,), (1,)), ((), ())), preferred_element_type=F32, precision=precision)


def _rms(x, g):
    ms = jnp.mean(x * x, axis=-1, keepdims=True)
    return x * lax.rsqrt(ms + EPS) * g


def _silu(x):
    return x * jax.nn.sigmoid(x)


def _norm_proj_kernel(x_ref, g_ref, w_ref, o_ref, xn_ref):
    @pl.when(pl.program_id(1) == 0)
    def _():
        xn_ref[...] = _rms(x_ref[...], g_ref[...]).astype(BF16)

    o_ref[...] = _dot(xn_ref[...], w_ref[...])


def _norm_glu_kernel(x_ref, g_ref, wa_ref, wg_ref, o_ref, xn_ref):
    @pl.when(pl.program_id(1) == 0)
    def _():
        xn_ref[...] = _rms(x_ref[...], g_ref[...]).astype(BF16)

    xn = xn_ref[...]
    a = _dot(xn, wa_ref[...])
    gate = _dot(xn, wg_ref[...])
    o_ref[...] = a * jax.nn.sigmoid(gate)


def norm_proj(x, g, w, *, tm, tn):
    m, d = x.shape
    n = w.shape[1]
    return pl.pallas_call(
        _norm_proj_kernel,
        out_shape=jax.ShapeDtypeStruct((m, n), F32),
        grid=(m // tm, n // tn),
        in_specs=[pl.BlockSpec((tm, d), lambda i, j: (i, 0)),
                  pl.BlockSpec((1, d), lambda i, j: (0, 0)),
                  pl.BlockSpec((d, tn), lambda i, j: (0, j))],
        out_specs=pl.BlockSpec((tm, tn), lambda i, j: (i, j)),
        scratch_shapes=[pltpu.VMEM((tm, d), BF16)],
        compiler_params=_cparams("parallel", "arbitrary"),
        name="norm_proj",
    )(x, g, w)


def norm_glu(x, g, w, *, tm, tn):
    m, d = x.shape
    n = w.shape[1] // 2
    nj = n // tn
    return pl.pallas_call(
        _norm_glu_kernel,
        out_shape=jax.ShapeDtypeStruct((m, n), F32),
        grid=(m // tm, nj),
        in_specs=[pl.BlockSpec((tm, d), lambda i, j: (i, 0)),
                  pl.BlockSpec((1, d), lambda i, j: (0, 0)),
                  pl.BlockSpec((d, tn), lambda i, j: (0, j)),
                  pl.BlockSpec((d, tn), lambda i, j: (0, j + nj))],
        out_specs=pl.BlockSpec((tm, tn), lambda i, j: (i, j)),
        scratch_shapes=[pltpu.VMEM((tm, d), BF16)],
        compiler_params=_cparams("parallel", "arbitrary"),
        name="norm_glu",
    )(x, g, w, w)


def _out_proj_kernel(*refs, n_in):
    a_refs = refs[:n_in]
    w_refs = refs[n_in:2 * n_in]
    x_ref, g_ref, o_ref = refs[2 * n_in:]
    mix = _dot(a_refs[0][...], w_refs[0][...])
    for a_ref, w_ref in zip(a_refs[1:], w_refs[1:]):
        mix = mix + _dot(a_ref[...], w_ref[...])
    o_ref[...] = x_ref[...] + _rms(mix, g_ref[...])


def out_proj(acts, ws, x, g, *, tm):
    m, d = x.shape
    n_in = len(acts)
    in_specs = ([pl.BlockSpec((tm, a.shape[1]), lambda i: (i, 0)) for a in acts]
                + [pl.BlockSpec(w.shape, lambda i: (0, 0)) for w in ws]
                + [pl.BlockSpec((tm, d), lambda i: (i, 0)), pl.BlockSpec((1, d), lambda i: (0, 0))])
    return pl.pallas_call(
        functools.partial(_out_proj_kernel, n_in=n_in),
        out_shape=jax.ShapeDtypeStruct((m, d), F32),
        grid=(m // tm,),
        in_specs=in_specs,
        out_specs=pl.BlockSpec((tm, d), lambda i: (i, 0)),
        compiler_params=_cparams("parallel"),
        name="out_proj",
    )(*acts, *ws, x, g)


def _mlp_kernel(x_ref, gpre_ref, wup_ref, wdn_ref, gpost_ref, o_ref, xn_ref, acc_ref):
    f = pl.program_id(1)

    @pl.when(f == 0)
    def _():
        xn_ref[...] = _rms(x_ref[...], gpre_ref[...]).astype(BF16)
        acc_ref[...] = jnp.zeros_like(acc_ref)

    h = _dot(xn_ref[...], wup_ref[...])
    h = jnp.square(jnp.maximum(h, 0.0)).astype(BF16)
    acc_ref[...] += _dot(h, wdn_ref[...])

    @pl.when(f == pl.num_programs(1) - 1)
    def _():
        o_ref[...] = x_ref[...] + _rms(acc_ref[...], gpost_ref[...])


def mlp(x, gpre, wup, wdn, gpost, *, tm, tf):
    m, d = x.shape
    ff = wup.shape[1]
    return pl.pallas_call(
        _mlp_kernel,
        out_shape=jax.ShapeDtypeStruct((m, d), F32),
        grid=(m // tm, ff // tf),
        in_specs=[pl.BlockSpec((tm, d), lambda i, f: (i, 0)),
                  pl.BlockSpec((1, d), lambda i, f: (0, 0)),
                  pl.BlockSpec((d, tf), lambda i, f: (0, f)),
                  pl.BlockSpec((tf, d), lambda i, f: (f, 0)),
                  pl.BlockSpec((1, d), lambda i, f: (0, 0))],
        out_specs=pl.BlockSpec((tm, d), lambda i, f: (i, 0)),
        scratch_shapes=[pltpu.VMEM((tm, d), BF16), pltpu.VMEM((tm, d), F32)],
        compiler_params=_cparams("parallel", "arbitrary"),
        name="mlp",
    )(x, gpre, wup, wdn, gpost)


def _qkv_prep_kernel(p_ref, cw_ref, q_ref, k_ref, v_ref, buf_ref, *, t, n_heads, halo):
    width = n_heads * LANES

    @pl.when(pl.program_id(1) == 0)
    def _():
        buf_ref[0:halo, :] = jnp.zeros((halo, 3 * width), F32)

    buf_ref[halo:halo + t, :] = p_ref[0]
    outs = (q_ref, k_ref, v_ref)
    for cg in range(3 * n_heads):
        cols = slice(cg * LANES, (cg + 1) * LANES)
        acc = None
        for j in range(SHORT_CONV):
            r0 = halo - (SHORT_CONV - 1) + j
            term = cw_ref[j:j + 1, cols] * buf_ref[r0:r0 + t, cols]
            acc = term if acc is None else acc + term
        y = _silu(acc)
        which, hh = divmod(cg, n_heads)
        if which < 2:
            y = y * lax.rsqrt(jnp.sum(y * y, axis=-1, keepdims=True) + EPS)
        if which == 0:
            y = y * (LANES ** -0.5)
        outs[which][0, :, hh * LANES:(hh + 1) * LANES] = y
    buf_ref[0:halo, :] = buf_ref[t:t + halo, :]


def qkv_prep(p3, conv_w, *, n_heads, t):
    b, s, _ = p3.shape
    width = n_heads * LANES
    halo = 8
    out = jax.ShapeDtypeStruct((b, s, width), F32)
    ospec = pl.BlockSpec((1, t, width), lambda bi, ti: (bi, ti, 0))
    return pl.pallas_call(
        functools.partial(_qkv_prep_kernel, t=t, n_heads=n_heads, halo=halo),
        out_shape=(out, out, out),
        grid=(b, s // t),
        in_specs=[pl.BlockSpec((1, t, 3 * width), lambda bi, ti: (bi, ti, 0)),
                  pl.BlockSpec((SHORT_CONV, 3 * width), lambda bi, ti: (0, 0))],
        out_specs=(ospec, ospec, ospec),
        scratch_shapes=[pltpu.VMEM((t + halo, 3 * width), F32)],
        compiler_params=_cparams("parallel", "arbitrary"),
        name="qkv_prep",
    )(p3, conv_w)


def _delta_kernel(q_ref, k_ref, v_ref, lg_ref, z_ref, alog_ref, dt_ref, dn_ref, o_ref,
                  state_ref, gcb_ref, bb_ref, gct_ref, bt_ref, u_ref, wq_ref, ak_ref, dec_ref, os_ref,
                  *, t, n_heads):
    c = DN_CHUNK
    nc = t // c
    heads = range(n_heads)

    @pl.when(pl.program_id(1) == 0)
    def _():
        state_ref[...] = jnp.zeros_like(state_ref)

    lg = lg_ref[0]
    beta_all = jax.nn.sigmoid(lg)
    sp_in = lg + dt_ref[...]
    softplus = jnp.maximum(sp_in, 0.0) + jnp.log1p(jnp.exp(-jnp.abs(sp_in)))
    g_all = -jnp.exp(alog_ref[...]) * softplus
    row = lax.broadcasted_iota(jnp.int32, (t, t), 0)
    col = lax.broadcasted_iota(jnp.int32, (t, t), 1)
    tri = ((row // c == col // c) & (col <= row)).astype(F32)
    gc_all = _dot(tri, g_all, HIGHEST)
    for h in heads:
        gcb_ref[h] = jnp.broadcast_to(gc_all[:, n_heads + h:n_heads + h + 1], (t, LANES))
        bb_ref[h] = jnp.broadcast_to(beta_all[:, h:h + 1], (t, LANES))

    ri = lax.broadcasted_iota(jnp.int32, (c, c), 0)
    ci = lax.broadcasted_iota(jnp.int32, (c, c), 1)
    eye = (ri == ci).astype(F32)
    eye16 = eye.astype(BF16)
    for ic in range(nc):
        gct_ref[ic] = _dot_nt(eye, gc_all[ic * c:(ic + 1) * c, :], HIGHEST)
        bt_ref[ic] = _dot_nt(eye, beta_all[ic * c:(ic + 1) * c, :], HIGHEST)

    def intra(ic, carry):
        rows = pl.ds(pl.multiple_of(ic * c, c), c)
        causal = ri >= ci
        blk = ri ^ ci
        a_mats, t_mats = [], []
        for h in heads:
            cols = slice(h * LANES, (h + 1) * LANES)
            k16 = k_ref[0, rows, cols].astype(BF16)
            q16 = q_ref[0, rows, cols].astype(BF16)
            qkk = _dot_nt(jnp.concatenate([q16, k16], axis=0), k16)
            gcr = jnp.broadcast_to(gct_ref[ic, n_heads + h:n_heads + h + 1, :], (c, c))
            gamma = jnp.exp(jnp.where(causal, gcb_ref[h, rows, :] - gcr, -1e30))
            ak_ref[h, ic, 0:c, :] = (qkk[0:c] * gamma).astype(BF16)
            a_mat = jnp.where(ri > ci, bb_ref[h, rows, :] * qkk[c:2 * c] * gamma, 0.0)
            a_mats.append(a_mat)
            t_mats.append(eye - jnp.where(blk == 1, a_mat, 0.0))
        s = 2
        while s < c:
            join = (blk >= s) & (blk < 2 * s)
            t16 = [tm.astype(BF16) for tm in t_mats]
            ys = [_dot(t16[h], jnp.where(join, a_mats[h], 0.0).astype(BF16)) for h in heads]
            t_mats = [t_mats[h] - _dot(ys[h].astype(BF16), t16[h]) for h in heads]
            s *= 2
        for h in heads:
            cols = slice(h * LANES, (h + 1) * LANES)
            gcb = gcb_ref[h, rows, :]
            kh = k_ref[0, rows, cols]
            e_col = jnp.exp(gcb)
            br = jnp.broadcast_to(bt_ref[ic, h:h + 1, :], (c, c))
            rhs = jnp.concatenate([v_ref[0, rows, cols].astype(BF16), (kh * e_col).astype(BF16)], axis=1)
            uw = _dot((t_mats[h] * br).astype(BF16), rhs)
            u_ref[rows, cols] = uw[:, 0:LANES]
            wq_ref[h, ic, 0:c, :] = uw[:, LANES:2 * LANES].astype(BF16)
            wq_ref[h, ic, c:2 * c, :] = (q_ref[0, rows, cols] * e_col).astype(BF16)
            g_last = gcb[c - 1:c, :]
            kd16 = (kh * jnp.exp(g_last - gcb)).astype(BF16)
            ak_ref[h, ic, c:2 * c, :] = _dot_nt(eye16, kd16).astype(BF16)
            dec_ref[ic, h:h + 1, :] = jnp.exp(g_last)
        return carry

    lax.fori_loop(0, nc, intra, 0)

    def inter(ic, carry):
        rows = pl.ds(pl.multiple_of(ic * c, c), c)
        sts, qss, vns = [], [], []
        for h in heads:
            cols = slice(h * LANES, (h + 1) * LANES)
            st = state_ref[h]
            wq = _dot(wq_ref[h, ic], st.astype(BF16))
            sts.append(st)
            vns.append((u_ref[rows, cols] - wq[0:c]).astype(BF16))
            qss.append(wq[c:2 * c])
        for h in heads:
            cols = slice(h * LANES, (h + 1) * LANES)
            ak = _dot(ak_ref[h, ic], vns[h])
            os_ref[rows, cols] = qss[h] + ak[0:c]
            state_ref[h] = sts[h] * dec_ref[ic, h:h + 1, :] + ak[c:2 * c]
        return carry

    lax.fori_loop(0, nc, inter, 0)

    for h in heads:
        cols = slice(h * LANES, (h + 1) * LANES)
        o_ref[0, :, cols] = (_rms(os_ref[:, cols], dn_ref[...]) * _silu(z_ref[0, :, cols])).astype(BF16)


def delta_rule(q, k, v, p3, alog_pad, dt_pad, dn_norm, *, n_heads, t, z_blk, lg_blk):
    b, s, width = q.shape
    c = DN_CHUNK
    nc = t // c
    assert width == n_heads * LANES and c == LANES and t % c == 0
    qspec = pl.BlockSpec((1, t, width), lambda bi, ti: (bi, ti, 0))
    row = pl.BlockSpec((1, LANES), lambda bi, ti: (0, 0))
    return pl.pallas_call(
        functools.partial(_delta_kernel, t=t, n_heads=n_heads),
        out_shape=jax.ShapeDtypeStruct((b, s, width), BF16),
        grid=(b, s // t),
        in_specs=[qspec, qspec, qspec,
                  pl.BlockSpec((1, t, LANES), lambda bi, ti: (bi, ti, lg_blk)),
                  pl.BlockSpec((1, t, width), lambda bi, ti: (bi, ti, z_blk)),
                  row, row, row],
        out_specs=qspec,
        scratch_shapes=[pltpu.VMEM((n_heads, LANES, LANES), F32),
                        pltpu.VMEM((n_heads, t, LANES), F32),
                        pltpu.VMEM((n_heads, t, LANES), F32),
                        pltpu.VMEM((nc, LANES, c), F32),
                        pltpu.VMEM((nc, LANES, c), F32),
                        pltpu.VMEM((t, width), F32),
                        pltpu.VMEM((n_heads, nc, 2 * c, LANES), BF16),
                        pltpu.VMEM((n_heads, nc, 2 * c, LANES), BF16),
                        pltpu.VMEM((nc, n_heads, LANES), F32),
                        pltpu.VMEM((t, width), F32)],
        compiler_params=_cparams("parallel", "arbitrary"),
        name="delta_rule",
    )(q, k, v, p3, p3, alog_pad, dt_pad, dn_norm)


def _pool_kernel(xp_ref, w_ref, sc_ref, o_ref, buf_ref, *, t, halo, gdim):
    ti = pl.program_id(1)

    @pl.when(ti == 0)
    def _():
        buf_ref[0:halo, :] = jnp.zeros((halo, buf_ref.shape[1]), F32)

    buf_ref[halo:halo + t, :] = xp_ref[0]
    pos = ti * t + lax.broadcasted_iota(jnp.int32, (t, 1), 0) + 1
    for gi, win in enumerate(POOL_WINDOWS):
        cols = slice(gi * gdim, (gi + 1) * gdim)
        acc = buf_ref[halo:halo + t, cols]
        tok = acc
        for d in range(1, win):
            acc = acc + buf_ref[halo - d:halo - d + t, cols]
        pooled = acc / jnp.minimum(pos, win).astype(F32) - tok
        y = _dot(pooled.astype(BF16), w_ref[gi])
        o_ref[0, :, cols] = (y * sc_ref[:, cols]).astype(BF16)
    buf_ref[0:halo, :] = buf_ref[t:t + halo, :]


def pool_mixer(p3, w_grp, scale, *, t, xp_blk):
    b, s, _ = p3.shape
    groups, gdim, _ = w_grp.shape
    width = groups * gdim
    halo = 16
    return pl.pallas_call(
        functools.partial(_pool_kernel, t=t, halo=halo, gdim=gdim),
        out_shape=jax.ShapeDtypeStruct((b, s, width), BF16),
        grid=(b, s // t),
        in_specs=[pl.BlockSpec((1, t, width), lambda bi, ti: (bi, ti, xp_blk)),
                  pl.BlockSpec((groups, gdim, gdim), lambda bi, ti: (0, 0, 0)),
                  pl.BlockSpec((1, width), lambda bi, ti: (0, 0))],
        out_specs=pl.BlockSpec((1, t, width), lambda bi, ti: (bi, ti, 0)),
        scratch_shapes=[pltpu.VMEM((t + halo, width), F32)],
        compiler_params=_cparams("parallel", "arbitrary"),
        name="pool_mixer",
    )(p3, w_grp, scale)


def _conf_conv_kernel(u_ref, w_ref, b_ref, g_ref, beta_ref, o_ref, buf_ref, y_ref, *, t, halo, kw):
    width = u_ref.shape[2]
    ncg = width // LANES
    half = t // 2

    @pl.when(pl.program_id(1) == 0)
    def _():
        buf_ref[:, 0:halo, :] = jnp.zeros((ncg, halo, LANES), F32)

    rsum = jnp.zeros((t, LANES), F32)
    for cg in range(ncg):
        cols = slice(cg * LANES, (cg + 1) * LANES)
        buf_ref[cg, halo:halo + t, :] = u_ref[0, :, cols]
        bias = jnp.broadcast_to(b_ref[:, cols], (half, LANES))
        acc_e, acc_o = bias, bias
        for j in range(kw):
            r0 = halo - (kw - 1) + j
            wj = w_ref[j:j + 1, cols]
            acc_e = acc_e + wj * buf_ref[cg, pl.ds(r0, half, stride=2), :]
            acc_o = acc_o + wj * buf_ref[cg, pl.ds(r0 + 1, half, stride=2), :]
        y_ref[cg, pl.ds(0, half, stride=2), :] = acc_e
        y_ref[cg, pl.ds(1, half, stride=2), :] = acc_o
        buf_ref[cg, 0:halo, :] = buf_ref[cg, t:t + halo, :]
        rsum = rsum + y_ref[cg]
    mu = jnp.sum(rsum, axis=-1, keepdims=True) * (1.0 / width)
    vsum = jnp.zeros((t, LANES), F32)
    for cg in range(ncg):
        yc = y_ref[cg] - mu
        vsum = vsum + yc * yc
    rstd = lax.rsqrt(jnp.sum(vsum, axis=-1, keepdims=True) * (1.0 / width) + EPS)
    for cg in range(ncg):
        cols = slice(cg * LANES, (cg + 1) * LANES)
        yn = (y_ref[cg] - mu) * rstd * g_ref[:, cols] + beta_ref[:, cols]
        o_ref[0, :, cols] = _silu(yn).astype(BF16)


def conf_conv(u3, dw, dw_b, ln_g, ln_b, *, t):
    b, s, width = u3.shape
    kw = dw.shape[0]
    halo = 32
    assert kw - 1 <= halo and t % 16 == 0
    row = pl.BlockSpec((1, width), lambda bi, ti: (0, 0))
    blk = pl.BlockSpec((1, t, width), lambda bi, ti: (bi, ti, 0))
    return pl.pallas_call(
        functools.partial(_conf_conv_kernel, t=t, halo=halo, kw=kw),
        out_shape=jax.ShapeDtypeStruct((b, s, width), BF16),
        grid=(b, s // t),
        in_specs=[blk, pl.BlockSpec((kw, width), lambda bi, ti: (0, 0)), row, row, row],
        out_specs=blk,
        scratch_shapes=[pltpu.VMEM((width // LANES, t + halo, LANES), F32),
                        pltpu.VMEM((width // LANES, t, LANES), F32)],
        compiler_params=_cparams("parallel", "arbitrary"),
        name="conf_conv",
    )(u3, dw, dw_b, ln_g, ln_b)


def _even_layer(x2, b, s, g_pre, g_post, w_in, conv_w, a_log, dt_bias, dn_norm, pool_w, pool_scale, w_out):
    d = x2.shape[1]
    n_heads = a_log.shape[0]
    dn_w = n_heads * dn_norm.shape[0]
    pool_width = pool_scale.shape[0]
    o2 = 4 * dn_w
    o4 = o2 + 2 * n_heads
    n_pad = 768 * pl.cdiv(4 * dn_w + pool_width + LANES, 768)
    w_cat = jnp.concatenate(
        [w_in[:, :o2], w_in[:, o4:], w_in[:, o2:o4],
         jnp.zeros((d, n_pad - w_in.shape[1]), w_in.dtype)], axis=1).astype(BF16)
    p = norm_proj(x2, g_pre.reshape(1, d), w_cat, tm=1024, tn=768)
    p3 = p.reshape(b, s, n_pad)
    q, k, v = qkv_prep(p3, conv_w, n_heads=n_heads, t=512)
    pad = jnp.zeros((n_heads,), F32)
    tail = jnp.zeros((LANES - 2 * n_heads,), F32)
    alog_pad = jnp.concatenate([pad, a_log, tail]).reshape(1, LANES)
    dt_pad = jnp.concatenate([pad, dt_bias, tail]).reshape(1, LANES)
    o = delta_rule(q, k, v, p3, alog_pad, dt_pad, dn_norm.reshape(1, LANES), n_heads=n_heads, t=512,
                   z_blk=3, lg_blk=(4 * dn_w + pool_width) // LANES)
    y_pool = pool_mixer(p3, pool_w.astype(BF16), pool_scale.reshape(1, pool_width), t=512,
                        xp_blk=4 * dn_w // pool_width)
    w_out16 = w_out.astype(BF16)
    return out_proj([o.reshape(b * s, dn_w), y_pool.reshape(b * s, pool_width)],
                    [w_out16[:dn_w], w_out16[dn_w:]], x2, g_post.reshape(1, d), tm=512)


def _odd_layer(x2, b, s, g_pre, g_post, w_in, dw, dw_b, ln_g, ln_b, w_out):
    d = x2.shape[1]
    width = w_out.shape[0]
    u = norm_glu(x2, g_pre.reshape(1, d), w_in.astype(BF16), tm=1024, tn=512)
    a = conf_conv(u.reshape(b, s, width), dw, dw_b.reshape(1, width), ln_g.reshape(1, width),
                  ln_b.reshape(1, width), t=256)
    return out_proj([a.reshape(b * s, width)], [w_out.astype(BF16)], x2, g_post.reshape(1, d), tm=512)


def kernel(x, norm_mix_pre, norm_mix_post, norm_mlp_pre, norm_mlp_post, even_w_in, even_conv, even_a_log, even_dt_bias, even_dn_norm, even_pool_w, even_pool_scale, even_w_out, odd_w_in, odd_dw, odd_dw_b, odd_ln_g, odd_ln_b, odd_w_out, mlp_w_up, mlp_w_down):
    b, s, d = x.shape
    depth = norm_mix_pre.shape[0]
    x2 = x.reshape(b * s, d)
    for i in range(depth):
        j = i // 2
        if i % 2 == 0:
            x2 = _even_layer(x2, b, s, norm_mix_pre[i], norm_mix_post[i], even_w_in[j], even_conv[j],
                             even_a_log[j], even_dt_bias[j], even_dn_norm[j], even_pool_w[j],
                             even_pool_scale[j], even_w_out[j])
        else:
            x2 = _odd_layer(x2, b, s, norm_mix_pre[i], norm_mix_post[i], odd_w_in[j], odd_dw[j],
                            odd_dw_b[j], odd_ln_g[j], odd_ln_b[j], odd_w_out[j])
        x2 = mlp(x2, norm_mlp_pre[i].reshape(1, d), mlp_w_up[i].astype(BF16), mlp_w_down[i].astype(BF16),
                 norm_mlp_post[i].reshape(1, d), tm=512, tf=512)
    return x2.reshape(b, s, d)
```

```python
import functools

import jax
import jax.numpy as jnp
from jax import lax
from jax.experimental import pallas as pl
from jax.experimental.pallas import tpu as pltpu

F32 = jnp.float32
BF16 = jnp.bfloat16
EPS = 1e-6
LANES = 128
DN_CHUNK = 128
SHORT_CONV = 4
POOL_WINDOWS = (2, 4, 8, 16)
VMEM_LIMIT = 56 * 1024 * 1024
HIGHEST = lax.Precision.HIGHEST


def _cparams(*sem):
    return pltpu.CompilerParams(dimension_semantics=sem, vmem_limit_bytes=VMEM_LIMIT)


def _dot(a, b, precision=None):
    return jnp.dot(a, b, preferred_element_type=F32, precision=precision)


def _dot_nt(a, b, precision=None):
    return lax.dot_general(a, b, (((1,), (1,)), ((), ())), preferred_element_type=F32, precision=precision)


def _rms(x, g):
    ms = jnp.mean(x * x, axis=-1, keepdims=True)
    return x * lax.rsqrt(ms + EPS) * g


def _silu(x):
    return x * jax.nn.sigmoid(x)


def _normed(x_ref, g_ref, xn_ref, body):
    j = pl.program_id(1)

    @pl.when(j == 0)
    def _():
        xn = _rms(x_ref[...], g_ref[...]).astype(BF16)
        xn_ref[...] = xn
        body(xn)

    @pl.when(j > 0)
    def _():
        body(xn_ref[...])


def _norm_proj_kernel(x_ref, g_ref, w_ref, o_ref, xn_ref):
    def body(xn):
        o_ref[...] = _dot(xn, w_ref[...])

    _normed(x_ref, g_ref, xn_ref, body)


def _norm_glu_kernel(x_ref, g_ref, wa_ref, wg_ref, o_ref, xn_ref):
    def body(xn):
        a = _dot(xn, wa_ref[...])
        gate = _dot(xn, wg_ref[...])
        o_ref[...] = a * jax.nn.sigmoid(gate)

    _normed(x_ref, g_ref, xn_ref, body)


def norm_proj(x, g, w, *, tm, tn):
    m, d = x.shape
    n = w.shape[1]
    return pl.pallas_call(
        _norm_proj_kernel,
        out_shape=jax.ShapeDtypeStruct((m, n), F32),
        grid=(m // tm, n // tn),
        in_specs=[pl.BlockSpec((tm, d), lambda i, j: (i, 0)),
                  pl.BlockSpec((1, d), lambda i, j: (0, 0)),
                  pl.BlockSpec((d, tn), lambda i, j: (0, j))],
        out_specs=pl.BlockSpec((tm, tn), lambda i, j: (i, j)),
        scratch_shapes=[pltpu.VMEM((tm, d), BF16)],
        compiler_params=_cparams("parallel", "arbitrary"),
        name="norm_proj",
    )(x, g, w)


def norm_glu(x, g, w, *, tm, tn):
    m, d = x.shape
    n = w.shape[1] // 2
    nj = n // tn
    return pl.pallas_call(
        _norm_glu_kernel,
        out_shape=jax.ShapeDtypeStruct((m, n), F32),
        grid=(m // tm, nj),
        in_specs=[pl.BlockSpec((tm, d), lambda i, j: (i, 0)),
                  pl.BlockSpec((1, d), lambda i, j: (0, 0)),
                  pl.BlockSpec((d, tn), lambda i, j: (0, j)),
                  pl.BlockSpec((d, tn), lambda i, j: (0, j + nj))],
        out_specs=pl.BlockSpec((tm, tn), lambda i, j: (i, j)),
        scratch_shapes=[pltpu.VMEM((tm, d), BF16)],
        compiler_params=_cparams("parallel", "arbitrary"),
        name="norm_glu",
    )(x, g, w, w)


def _out_proj_kernel(*refs, n_in):
    a_refs = refs[:n_in]
    w_refs = refs[n_in:2 * n_in]
    x_ref, g_ref, o_ref = refs[2 * n_in:]
    mix = _dot(a_refs[0][...], w_refs[0][...])
    for a_ref, w_ref in zip(a_refs[1:], w_refs[1:]):
        mix = mix + _dot(a_ref[...], w_ref[...])
    o_ref[...] = x_ref[...] + _rms(mix, g_ref[...])


def out_proj(acts, ws, x, g, *, tm):
    m, d = x.shape
    n_in = len(acts)
    in_specs = ([pl.BlockSpec((tm, a.shape[1]), lambda i: (i, 0)) for a in acts]
                + [pl.BlockSpec(w.shape, lambda i: (0, 0)) for w in ws]
                + [pl.BlockSpec((tm, d), lambda i: (i, 0)), pl.BlockSpec((1, d), lambda i: (0, 0))])
    return pl.pallas_call(
        functools.partial(_out_proj_kernel, n_in=n_in),
        out_shape=jax.ShapeDtypeStruct((m, d), F32),
        grid=(m // tm,),
        in_specs=in_specs,
        out_specs=pl.BlockSpec((tm, d), lambda i: (i, 0)),
        compiler_params=_cparams("parallel"),
        name="out_proj",
    )(*acts, *ws, x, g)


def _mlp_kernel(x_ref, gpre_ref, wup_ref, wdn_ref, gpost_ref, o_ref, xn_ref, acc_ref):
    f = pl.program_id(1)
    last = pl.num_programs(1) - 1

    def ff_part(xn):
        h = _dot(xn, wup_ref[...])
        h = jnp.square(jnp.maximum(h, 0.0)).astype(BF16)
        return _dot(h, wdn_ref[...])

    @pl.when(f == 0)
    def _():
        xn = _rms(x_ref[...], gpre_ref[...]).astype(BF16)
        xn_ref[...] = xn
        acc_ref[...] = ff_part(xn)

    @pl.when((f > 0) & (f < last))
    def _():
        acc_ref[...] += ff_part(xn_ref[...])

    @pl.when(f == last)
    def _():
        o_ref[...] = x_ref[...] + _rms(acc_ref[...] + ff_part(xn_ref[...]), gpost_ref[...])


def mlp(x, gpre, wup, wdn, gpost, *, layer, tm, tf):
    m, d = x.shape
    ff = wup.shape[2]
    assert ff // tf >= 2
    return pl.pallas_call(
        _mlp_kernel,
        out_shape=jax.ShapeDtypeStruct((m, d), F32),
        grid=(m // tm, ff // tf),
        in_specs=[pl.BlockSpec((tm, d), lambda i, f: (i, 0)),
                  pl.BlockSpec((1, d), lambda i, f: (0, 0)),
                  pl.BlockSpec((None, d, tf), lambda i, f: (layer, 0, f)),
                  pl.BlockSpec((None, tf, d), lambda i, f: (layer, f, 0)),
                  pl.BlockSpec((1, d), lambda i, f: (0, 0))],
        out_specs=pl.BlockSpec((tm, d), lambda i, f: (i, 0)),
        scratch_shapes=[pltpu.VMEM((tm, d), BF16), pltpu.VMEM((tm, d), F32)],
        compiler_params=_cparams("parallel", "arbitrary"),
        name="mlp",
    )(x, gpre, wup, wdn, gpost)


def _qkv_prep_kernel(p_ref, cw_ref, q_ref, k_ref, v_ref, buf_ref, *, t, n_heads, halo):
    width = n_heads * LANES

    @pl.when(pl.program_id(1) == 0)
    def _():
        buf_ref[0:halo, :] = jnp.zeros((halo, 3 * width), F32)

    buf_ref[halo:halo + t, :] = p_ref[0]
    outs = (q_ref, k_ref, v_ref)
    for cg in range(3 * n_heads):
        cols = slice(cg * LANES, (cg + 1) * LANES)
        acc = None
        for j in range(SHORT_CONV):
            r0 = halo - (SHORT_CONV - 1) + j
            term = cw_ref[j:j + 1, cols] * buf_ref[r0:r0 + t, cols]
            acc = term if acc is None else acc + term
        y = _silu(acc)
        which, hh = divmod(cg, n_heads)
        if which < 2:
            y = y * lax.rsqrt(jnp.sum(y * y, axis=-1, keepdims=True) + EPS)
        if which == 0:
            y = y * (LANES ** -0.5)
        outs[which][0, :, hh * LANES:(hh + 1) * LANES] = y
    buf_ref[0:halo, :] = buf_ref[t:t + halo, :]


def qkv_prep(p3, conv_w, *, n_heads, t):
    b, s, _ = p3.shape
    width = n_heads * LANES
    halo = 8
    out = jax.ShapeDtypeStruct((b, s, width), F32)
    ospec = pl.BlockSpec((1, t, width), lambda bi, ti: (bi, ti, 0))
    return pl.pallas_call(
        functools.partial(_qkv_prep_kernel, t=t, n_heads=n_heads, halo=halo),
        out_shape=(out, out, out),
        grid=(b, s // t),
        in_specs=[pl.BlockSpec((1, t, 3 * width), lambda bi, ti: (bi, ti, 0)),
                  pl.BlockSpec((SHORT_CONV, 3 * width), lambda bi, ti: (0, 0))],
        out_specs=(ospec, ospec, ospec),
        scratch_shapes=[pltpu.VMEM((t + halo, 3 * width), F32)],
        compiler_params=_cparams("parallel", "arbitrary"),
        name="qkv_prep",
    )(p3, conv_w)


def _delta_kernel(q_ref, k_ref, v_ref, lg_ref, z_ref, alog_ref, dt_ref, dn_ref, o_ref,
                  state_ref, gcb_ref, bb_ref, gct_ref, bt_ref, u_ref, wq_ref, ak_ref, dec_ref, os_ref,
                  *, t, n_heads):
    c = DN_CHUNK
    nc = t // c
    heads = range(n_heads)

    @pl.when(pl.program_id(1) == 0)
    def _():
        state_ref[...] = jnp.zeros_like(state_ref)

    lg = lg_ref[0]
    beta_all = jax.nn.sigmoid(lg)
    sp_in = lg + dt_ref[...]
    softplus = jnp.maximum(sp_in, 0.0) + jnp.log1p(jnp.exp(-jnp.abs(sp_in)))
    g_all = -jnp.exp(alog_ref[...]) * softplus
    row = lax.broadcasted_iota(jnp.int32, (t, t), 0)
    col = lax.broadcasted_iota(jnp.int32, (t, t), 1)
    tri = ((row // c == col // c) & (col <= row)).astype(F32)
    gc_all = _dot(tri, g_all, HIGHEST)
    for h in heads:
        gcb_ref[h] = jnp.broadcast_to(gc_all[:, n_heads + h:n_heads + h + 1], (t, LANES))
        bb_ref[h] = jnp.broadcast_to(beta_all[:, h:h + 1], (t, LANES))

    ri = lax.broadcasted_iota(jnp.int32, (c, c), 0)
    ci = lax.broadcasted_iota(jnp.int32, (c, c), 1)
    eye = (ri == ci).astype(F32)
    eye16 = eye.astype(BF16)
    for ic in range(nc):
        gct_ref[ic] = _dot_nt(eye, gc_all[ic * c:(ic + 1) * c, :], HIGHEST)
        bt_ref[ic] = _dot_nt(eye, beta_all[ic * c:(ic + 1) * c, :], HIGHEST)

    def intra(ic, carry):
        rows = pl.ds(pl.multiple_of(ic * c, c), c)
        causal = ri >= ci
        blk = ri ^ ci
        a_mats, t_mats = [], []
        for h in heads:
            cols = slice(h * LANES, (h + 1) * LANES)
            k16 = k_ref[0, rows, cols].astype(BF16)
            q16 = q_ref[0, rows, cols].astype(BF16)
            qkk = _dot_nt(jnp.concatenate([q16, k16], axis=0), k16)
            gcr = jnp.broadcast_to(gct_ref[ic, n_heads + h:n_heads + h + 1, :], (c, c))
            gamma = jnp.exp(jnp.where(causal, gcb_ref[h, rows, :] - gcr, -1e30))
            ak_ref[h, ic, 0:c, :] = (qkk[0:c] * gamma).astype(BF16)
            a_mat = jnp.where(ri > ci, bb_ref[h, rows, :] * qkk[c:2 * c] * gamma, 0.0)
            a_mats.append(a_mat)
            t_mats.append(eye - jnp.where(blk == 1, a_mat, 0.0))
        s = 2
        while s < c:
            join = (blk >= s) & (blk < 2 * s)
            t16 = [tm.astype(BF16) for tm in t_mats]
            ys = [_dot(t16[h], jnp.where(join, a_mats[h], 0.0).astype(BF16)) for h in heads]
            t_mats = [t_mats[h] - _dot(ys[h].astype(BF16), t16[h]) for h in heads]
            s *= 2
        for h in heads:
            cols = slice(h * LANES, (h + 1) * LANES)
            gcb = gcb_ref[h, rows, :]
            kh = k_ref[0, rows, cols]
            e_col = jnp.exp(gcb)
            br = jnp.broadcast_to(bt_ref[ic, h:h + 1, :], (c, c))
            rhs = jnp.concatenate([v_ref[0, rows, cols].astype(BF16), (kh * e_col).astype(BF16)], axis=1)
            uw = _dot((t_mats[h] * br).astype(BF16), rhs)
            u_ref[rows, cols] = uw[:, 0:LANES]
            wq_ref[h, ic, 0:c, :] = uw[:, LANES:2 * LANES].astype(BF16)
            wq_ref[h, ic, c:2 * c, :] = (q_ref[0, rows, cols] * e_col).astype(BF16)
            g_last = gcb[c - 1:c, :]
            kd16 = (kh * jnp.exp(g_last - gcb)).astype(BF16)
            ak_ref[h, ic, c:2 * c, :] = _dot_nt(eye16, kd16).astype(BF16)
            dec_ref[ic, h:h + 1, :] = jnp.exp(g_last)
        return carry

    lax.fori_loop(0, nc, intra, 0)

    def inter(ic, carry):
        rows = pl.ds(pl.multiple_of(ic * c, c), c)
        sts, qss, vns = [], [], []
        for h in heads:
            cols = slice(h * LANES, (h + 1) * LANES)
            st = state_ref[h]
            wq = _dot(wq_ref[h, ic], st.astype(BF16))
            sts.append(st)
            vns.append((u_ref[rows, cols] - wq[0:c]).astype(BF16))
            qss.append(wq[c:2 * c])
        for h in heads:
            cols = slice(h * LANES, (h + 1) * LANES)
            ak = _dot(ak_ref[h, ic], vns[h])
            os_ref[rows, cols] = qss[h] + ak[0:c]
            state_ref[h] = sts[h] * dec_ref[ic, h:h + 1, :] + ak[c:2 * c]
        return carry

    lax.fori_loop(0, nc, inter, 0)

    for h in heads:
        cols = slice(h * LANES, (h + 1) * LANES)
        o_ref[0, :, cols] = (_rms(os_ref[:, cols], dn_ref[...]) * _silu(z_ref[0, :, cols])).astype(BF16)


def delta_rule(q, k, v, p3, alog_pad, dt_pad, dn_norm, *, n_heads, t, z_blk, lg_blk):
    b, s, width = q.shape
    c = DN_CHUNK
    nc = t // c
    assert width == n_heads * LANES and c == LANES and t % c == 0
    qspec = pl.BlockSpec((1, t, width), lambda bi, ti: (bi, ti, 0))
    row = pl.BlockSpec((1, LANES), lambda bi, ti: (0, 0))
    return pl.pallas_call(
        functools.partial(_delta_kernel, t=t, n_heads=n_heads),
        out_shape=jax.ShapeDtypeStruct((b, s, width), BF16),
        grid=(b, s // t),
        in_specs=[qspec, qspec, qspec,
                  pl.BlockSpec((1, t, LANES), lambda bi, ti: (bi, ti, lg_blk)),
                  pl.BlockSpec((1, t, width), lambda bi, ti: (bi, ti, z_blk)),
                  row, row, row],
        out_specs=qspec,
        scratch_shapes=[pltpu.VMEM((n_heads, LANES, LANES), F32),
                        pltpu.VMEM((n_heads, t, LANES), F32),
                        pltpu.VMEM((n_heads, t, LANES), F32),
                        pltpu.VMEM((nc, LANES, c), F32),
                        pltpu.VMEM((nc, LANES, c), F32),
                        pltpu.VMEM((t, width), F32),
                        pltpu.VMEM((n_heads, nc, 2 * c, LANES), BF16),
                        pltpu.VMEM((n_heads, nc, 2 * c, LANES), BF16),
                        pltpu.VMEM((nc, n_heads, LANES), F32),
                        pltpu.VMEM((t, width), F32)],
        compiler_params=_cparams("parallel", "arbitrary"),
        name="delta_rule",
    )(q, k, v, p3, p3, alog_pad, dt_pad, dn_norm)


def _pool_kernel(xp_ref, w_ref, sc_ref, o_ref, buf_ref, *, t, halo, gdim):
    ti = pl.program_id(1)

    @pl.when(ti == 0)
    def _():
        buf_ref[0:halo, :] = jnp.zeros((halo, buf_ref.shape[1]), F32)

    buf_ref[halo:halo + t, :] = xp_ref[0]
    pos = ti * t + lax.broadcasted_iota(jnp.int32, (t, 1), 0) + 1
    for gi, win in enumerate(POOL_WINDOWS):
        cols = slice(gi * gdim, (gi + 1) * gdim)
        acc = buf_ref[halo:halo + t, cols]
        tok = acc
        for d in range(1, win):
            acc = acc + buf_ref[halo - d:halo - d + t, cols]
        pooled = acc / jnp.minimum(pos, win).astype(F32) - tok
        y = _dot(pooled.astype(BF16), w_ref[gi])
        o_ref[0, :, cols] = (y * sc_ref[:, cols]).astype(BF16)
    buf_ref[0:halo, :] = buf_ref[t:t + halo, :]


def pool_mixer(p3, w_grp, scale, *, t, xp_blk):
    b, s, _ = p3.shape
    groups, gdim, _ = w_grp.shape
    width = groups * gdim
    halo = 16
    return pl.pallas_call(
        functools.partial(_pool_kernel, t=t, halo=halo, gdim=gdim),
        out_shape=jax.ShapeDtypeStruct((b, s, width), BF16),
        grid=(b, s // t),
        in_specs=[pl.BlockSpec((1, t, width), lambda bi, ti: (bi, ti, xp_blk)),
                  pl.BlockSpec((groups, gdim, gdim), lambda bi, ti: (0, 0, 0)),
                  pl.BlockSpec((1, width), lambda bi, ti: (0, 0))],
        out_specs=pl.BlockSpec((1, t, width), lambda bi, ti: (bi, ti, 0)),
        scratch_shapes=[pltpu.VMEM((t + halo, width), F32)],
        compiler_params=_cparams("parallel", "arbitrary"),
        name="pool_mixer",
    )(p3, w_grp, scale)


def _conf_conv_kernel(u_ref, w_ref, b_ref, g_ref, beta_ref, o_ref, buf_ref, y_ref, *, t, halo, kw):
    width = u_ref.shape[2]
    ncg = width // LANES
    half = t // 2

    @pl.when(pl.program_id(1) == 0)
    def _():
        buf_ref[:, 0:halo, :] = jnp.zeros((ncg, halo, LANES), F32)

    rsum = jnp.zeros((t, LANES), F32)
    for cg in range(ncg):
        cols = slice(cg * LANES, (cg + 1) * LANES)
        buf_ref[cg, halo:halo + t, :] = u_ref[0, :, cols]
        bias = jnp.broadcast_to(b_ref[:, cols], (half, LANES))
        acc_e, acc_o = bias, bias
        for j in range(kw):
            r0 = halo - (kw - 1) + j
            wj = w_ref[j:j + 1, cols]
            acc_e = acc_e + wj * buf_ref[cg, pl.ds(r0, half, stride=2), :]
            acc_o = acc_o + wj * buf_ref[cg, pl.ds(r0 + 1, half, stride=2), :]
        y_ref[cg, pl.ds(0, half, stride=2), :] = acc_e
        y_ref[cg, pl.ds(1, half, stride=2), :] = acc_o
        buf_ref[cg, 0:halo, :] = buf_ref[cg, t:t + halo, :]
        rsum = rsum + y_ref[cg]
    mu = jnp.sum(rsum, axis=-1, keepdims=True) * (1.0 / width)
    vsum = jnp.zeros((t, LANES), F32)
    for cg in range(ncg):
        yc = y_ref[cg] - mu
        vsum = vsum + yc * yc
    rstd = lax.rsqrt(jnp.sum(vsum, axis=-1, keepdims=True) * (1.0 / width) + EPS)
    for cg in range(ncg):
        cols = slice(cg * LANES, (cg + 1) * LANES)
        yn = (y_ref[cg] - mu) * rstd * g_ref[:, cols] + beta_ref[:, cols]
        o_ref[0, :, cols] = _silu(yn).astype(BF16)


def conf_conv(u3, dw, dw_b, ln_g, ln_b, *, t):
    b, s, width = u3.shape
    kw = dw.shape[0]
    halo = 32
    assert kw - 1 <= halo and t % 16 == 0
    row = pl.BlockSpec((1, width), lambda bi, ti: (0, 0))
    blk = pl.BlockSpec((1, t, width), lambda bi, ti: (bi, ti, 0))
    return pl.pallas_call(
        functools.partial(_conf_conv_kernel, t=t, halo=halo, kw=kw),
        out_shape=jax.ShapeDtypeStruct((b, s, width), BF16),
        grid=(b, s // t),
        in_specs=[blk, pl.BlockSpec((kw, width), lambda bi, ti: (0, 0)), row, row, row],
        out_specs=blk,
        scratch_shapes=[pltpu.VMEM((width // LANES, t + halo, LANES), F32),
                        pltpu.VMEM((width // LANES, t, LANES), F32)],
        compiler_params=_cparams("parallel", "arbitrary"),
        name="conf_conv",
    )(u3, dw, dw_b, ln_g, ln_b)


def _even_layer(x2, b, s, g_pre, g_post, w_in, conv_w, a_log, dt_bias, dn_norm, pool_w, pool_scale, w_out):
    d = x2.shape[1]
    n_heads = a_log.shape[0]
    dn_w = n_heads * dn_norm.shape[0]
    pool_width = pool_scale.shape[0]
    o2 = 4 * dn_w
    o4 = o2 + 2 * n_heads
    n_pad = 768 * pl.cdiv(4 * dn_w + pool_width + LANES, 768)
    w_cat = jnp.concatenate(
        [w_in[:, :o2], w_in[:, o4:], w_in[:, o2:o4],
         jnp.zeros((d, n_pad - w_in.shape[1]), w_in.dtype)], axis=1).astype(BF16)
    p = norm_proj(x2, g_pre.reshape(1, d), w_cat, tm=1024, tn=768)
    p3 = p.reshape(b, s, n_pad)
    q, k, v = qkv_prep(p3, conv_w, n_heads=n_heads, t=512)
    pad = jnp.zeros((n_heads,), F32)
    tail = jnp.zeros((LANES - 2 * n_heads,), F32)
    alog_pad = jnp.concatenate([pad, a_log, tail]).reshape(1, LANES)
    dt_pad = jnp.concatenate([pad, dt_bias, tail]).reshape(1, LANES)
    o = delta_rule(q, k, v, p3, alog_pad, dt_pad, dn_norm.reshape(1, LANES), n_heads=n_heads, t=512,
                   z_blk=3, lg_blk=(4 * dn_w + pool_width) // LANES)
    y_pool = pool_mixer(p3, pool_w.astype(BF16), pool_scale.reshape(1, pool_width), t=512,
                        xp_blk=4 * dn_w // pool_width)
    w_out16 = w_out.astype(BF16)
    return out_proj([o.reshape(b * s, dn_w), y_pool.reshape(b * s, pool_width)],
                    [w_out16[:dn_w], w_out16[dn_w:]], x2, g_post.reshape(1, d), tm=512)


def _odd_layer(x2, b, s, g_pre, g_post, w_in, dw, dw_b, ln_g, ln_b, w_out):
    d = x2.shape[1]
    width = w_out.shape[0]
    u = norm_glu(x2, g_pre.reshape(1, d), w_in.astype(BF16), tm=1024, tn=512)
    a = conf_conv(u.reshape(b, s, width), dw, dw_b.reshape(1, width), ln_g.reshape(1, width),
                  ln_b.reshape(1, width), t=256)
    return out_proj([a.reshape(b * s, width)], [w_out.astype(BF16)], x2, g_post.reshape(1, d), tm=512)


def kernel(x, norm_mix_pre, norm_mix_post, norm_mlp_pre, norm_mlp_post, even_w_in, even_conv, even_a_log, even_dt_bias, even_dn_norm, even_pool_w, even_pool_scale, even_w_out, odd_w_in, odd_dw, odd_dw_b, odd_ln_g, odd_ln_b, odd_w_out, mlp_w_up, mlp_w_down):
    b, s, d = x.shape
    depth = norm_mix_pre.shape[0]
    x2 = x.reshape(b * s, d)
    w_up16 = mlp_w_up.astype(BF16)
    w_down16 = mlp_w_down.astype(BF16)
    for i in range(depth):
        j = i // 2
        if i % 2 == 0:
            x2 = _even_layer(x2, b, s, norm_mix_pre[i], norm_mix_post[i], even_w_in[j], even_conv[j],
                             even_a_log[j], even_dt_bias[j], even_dn_norm[j], even_pool_w[j],
                             even_pool_scale[j], even_w_out[j])
        else:
            x2 = _odd_layer(x2, b, s, norm_mix_pre[i], norm_mix_post[i], odd_w_in[j], odd_dw[j],
                            odd_dw_b[j], odd_ln_g[j], odd_ln_b[j], odd_w_out[j])
        x2 = mlp(x2, norm_mlp_pre[i].reshape(1, d), w_up16, w_down16, norm_mlp_post[i].reshape(1, d),
                 layer=i, tm=512, tf=1024)
    return x2.reshape(b, s, d)
```

```python
import functools

import jax
import jax.numpy as jnp
from jax import lax
from jax.experimental import pallas as pl
from jax.experimental.pallas import tpu as pltpu

F32 = jnp.float32
BF16 = jnp.bfloat16
EPS = 1e-6
LANES = 128
DN_CHUNK = 128
SHORT_CONV = 4
POOL_WINDOWS = (2, 4, 8, 16)
VMEM_LIMIT = 56 * 1024 * 1024
HIGHEST = lax.Precision.HIGHEST


def _cparams(*sem):
    return pltpu.CompilerParams(dimension_semantics=sem, vmem_limit_bytes=VMEM_LIMIT)


def _dot(a, b, precision=None):
    return jnp.dot(a, b, preferred_element_type=F32, precision=precision)


def _dot_nt(a, b, precision=None):
    return lax.dot_general(a, b, (((1,), (1,)), ((), ())), preferred_element_type=F32, precision=precision)


def _rms(x, g):
    ms = jnp.mean(x * x, axis=-1, keepdims=True)
    return x * lax.rsqrt(ms + EPS) * g


def _silu(x):
    return x * jax.nn.sigmoid(x)


def _normed(x_ref, g_ref, xn_ref, body):
    j = pl.program_id(1)

    @pl.when(j == 0)
    def _():
        xn = _rms(x_ref[...], g_ref[...]).astype(BF16)
        xn_ref[...] = xn
        body(xn)

    @pl.when(j > 0)
    def _():
        body(xn_ref[...])


def _norm_proj_kernel(x_ref, g_ref, w_ref, o_ref, xn_ref):
    def body(xn):
        o_ref[...] = _dot(xn, w_ref[...])

    _normed(x_ref, g_ref, xn_ref, body)


def _norm_glu_kernel(x_ref, g_ref, wa_ref, wg_ref, o_ref, xn_ref):
    def body(xn):
        a = _dot(xn, wa_ref[...])
        gate = _dot(xn, wg_ref[...])
        o_ref[...] = a * jax.nn.sigmoid(gate)

    _normed(x_ref, g_ref, xn_ref, body)


def norm_proj(x, g, w, *, tm, tn):
    m, d = x.shape
    n = w.shape[1]
    return pl.pallas_call(
        _norm_proj_kernel,
        out_shape=jax.ShapeDtypeStruct((m, n), F32),
        grid=(m // tm, n // tn),
        in_specs=[pl.BlockSpec((tm, d), lambda i, j: (i, 0)),
                  pl.BlockSpec((1, d), lambda i, j: (0, 0)),
                  pl.BlockSpec((d, tn), lambda i, j: (0, j))],
        out_specs=pl.BlockSpec((tm, tn), lambda i, j: (i, j)),
        scratch_shapes=[pltpu.VMEM((tm, d), BF16)],
        compiler_params=_cparams("parallel", "arbitrary"),
        name="norm_proj",
    )(x, g, w)


def norm_glu(x, g, w, *, tm, tn):
    m, d = x.shape
    n = w.shape[1] // 2
    nj = n // tn
    return pl.pallas_call(
        _norm_glu_kernel,
        out_shape=jax.ShapeDtypeStruct((m, n), F32),
        grid=(m // tm, nj),
        in_specs=[pl.BlockSpec((tm, d), lambda i, j: (i, 0)),
                  pl.BlockSpec((1, d), lambda i, j: (0, 0)),
                  pl.BlockSpec((d, tn), lambda i, j: (0, j)),
                  pl.BlockSpec((d, tn), lambda i, j: (0, j + nj))],
        out_specs=pl.BlockSpec((tm, tn), lambda i, j: (i, j)),
        scratch_shapes=[pltpu.VMEM((tm, d), BF16)],
        compiler_params=_cparams("parallel", "arbitrary"),
        name="norm_glu",
    )(x, g, w, w)


def _out_proj_kernel(*refs, n_in):
    a_refs = refs[:n_in]
    w_refs = refs[n_in:2 * n_in]
    x_ref, g_ref, o_ref = refs[2 * n_in:]
    mix = _dot(a_refs[0][...], w_refs[0][...])
    for a_ref, w_ref in zip(a_refs[1:], w_refs[1:]):
        mix = mix + _dot(a_ref[...], w_ref[...])
    o_ref[...] = x_ref[...] + _rms(mix, g_ref[...])


def out_proj(acts, ws, x, g, *, tm):
    m, d = x.shape
    n_in = len(acts)
    in_specs = ([pl.BlockSpec((tm, a.shape[1]), lambda i: (i, 0)) for a in acts]
                + [pl.BlockSpec(w.shape, lambda i: (0, 0)) for w in ws]
                + [pl.BlockSpec((tm, d), lambda i: (i, 0)), pl.BlockSpec((1, d), lambda i: (0, 0))])
    return pl.pallas_call(
        functools.partial(_out_proj_kernel, n_in=n_in),
        out_shape=jax.ShapeDtypeStruct((m, d), F32),
        grid=(m // tm,),
        in_specs=in_specs,
        out_specs=pl.BlockSpec((tm, d), lambda i: (i, 0)),
        compiler_params=_cparams("parallel"),
        name="out_proj",
    )(*acts, *ws, x, g)


def _mlp_kernel(x_ref, gpre_ref, wup_ref, wdn_ref, gpost_ref, o_ref, xn_ref, acc_ref):
    f = pl.program_id(1)
    last = pl.num_programs(1) - 1

    def ff_part(xn):
        h = _dot(xn, wup_ref[...])
        h = jnp.square(jnp.maximum(h, 0.0)).astype(BF16)
        return _dot(h, wdn_ref[...])

    @pl.when(f == 0)
    def _():
        xn = _rms(x_ref[...], gpre_ref[...]).astype(BF16)
        xn_ref[...] = xn
        acc_ref[...] = ff_part(xn)

    @pl.when((f > 0) & (f < last))
    def _():
        acc_ref[...] += ff_part(xn_ref[...])

    @pl.when(f == last)
    def _():
        o_ref[...] = x_ref[...] + _rms(acc_ref[...] + ff_part(xn_ref[...]), gpost_ref[...])


def mlp(x, gpre, wup, wdn, gpost, *, layer, tm, tf):
    m, d = x.shape
    ff = wup.shape[2]
    assert ff // tf >= 2
    return pl.pallas_call(
        _mlp_kernel,
        out_shape=jax.ShapeDtypeStruct((m, d), F32),
        grid=(m // tm, ff // tf),
        in_specs=[pl.BlockSpec((tm, d), lambda i, f: (i, 0)),
                  pl.BlockSpec((1, d), lambda i, f: (0, 0)),
                  pl.BlockSpec((None, d, tf), lambda i, f: (layer, 0, f)),
                  pl.BlockSpec((None, tf, d), lambda i, f: (layer, f, 0)),
                  pl.BlockSpec((1, d), lambda i, f: (0, 0))],
        out_specs=pl.BlockSpec((tm, d), lambda i, f: (i, 0)),
        scratch_shapes=[pltpu.VMEM((tm, d), BF16), pltpu.VMEM((tm, d), F32)],
        compiler_params=_cparams("parallel", "arbitrary"),
        name="mlp",
    )(x, gpre, wup, wdn, gpost)


def _qkv_prep_kernel(p_ref, cw_ref, q_ref, k_ref, v_ref, buf_ref, y_ref, *, t, n_heads, halo):
    ncg = 3 * n_heads
    half = t // 2

    @pl.when(pl.program_id(1) == 0)
    def _():
        buf_ref[:, 0:halo, :] = jnp.zeros((ncg, halo, LANES), F32)

    outs = (q_ref, k_ref, v_ref)
    for cg in range(ncg):
        cols = slice(cg * LANES, (cg + 1) * LANES)
        which, hh = divmod(cg, n_heads)
        buf_ref[cg, halo:halo + t, :] = p_ref[0, :, cols]
        for par in range(2):
            acc = None
            for j in range(SHORT_CONV):
                r0 = halo - (SHORT_CONV - 1) + j + par
                term = cw_ref[j:j + 1, cols] * buf_ref[cg, pl.ds(r0, half, stride=2), :]
                acc = term if acc is None else acc + term
            y = _silu(acc)
            if which < 2:
                y = y * lax.rsqrt(jnp.sum(y * y, axis=-1, keepdims=True) + EPS)
            if which == 0:
                y = y * (LANES ** -0.5)
            y_ref[cg, pl.ds(par, half, stride=2), :] = y
        buf_ref[cg, 0:halo, :] = buf_ref[cg, t:t + halo, :]
        outs[which][0, :, hh * LANES:(hh + 1) * LANES] = y_ref[cg]


def qkv_prep(p3, conv_w, *, n_heads, t):
    b, s, _ = p3.shape
    width = n_heads * LANES
    halo = 8
    out = jax.ShapeDtypeStruct((b, s, width), F32)
    ospec = pl.BlockSpec((1, t, width), lambda bi, ti: (bi, ti, 0))
    return pl.pallas_call(
        functools.partial(_qkv_prep_kernel, t=t, n_heads=n_heads, halo=halo),
        out_shape=(out, out, out),
        grid=(b, s // t),
        in_specs=[pl.BlockSpec((1, t, 3 * width), lambda bi, ti: (bi, ti, 0)),
                  pl.BlockSpec((SHORT_CONV, 3 * width), lambda bi, ti: (0, 0))],
        out_specs=(ospec, ospec, ospec),
        scratch_shapes=[pltpu.VMEM((3 * n_heads, t + halo, LANES), F32),
                        pltpu.VMEM((3 * n_heads, t, LANES), F32)],
        compiler_params=_cparams("parallel", "arbitrary"),
        name="qkv_prep",
    )(p3, conv_w)


def _delta_kernel(q_ref, k_ref, v_ref, lg_ref, z_ref, alog_ref, dt_ref, dn_ref, o_ref,
                  state_ref, gcb_ref, bb_ref, gct_ref, bt_ref, u_ref, wq_ref, ak_ref, dec_ref, os_ref,
                  *, t, n_heads):
    c = DN_CHUNK
    nc = t // c
    heads = range(n_heads)

    @pl.when(pl.program_id(1) == 0)
    def _():
        state_ref[...] = jnp.zeros_like(state_ref)

    lg = lg_ref[0]
    beta_all = jax.nn.sigmoid(lg)
    sp_in = lg + dt_ref[...]
    softplus = jnp.maximum(sp_in, 0.0) + jnp.log1p(jnp.exp(-jnp.abs(sp_in)))
    g_all = -jnp.exp(alog_ref[...]) * softplus
    ri = lax.broadcasted_iota(jnp.int32, (c, c), 0)
    ci = lax.broadcasted_iota(jnp.int32, (c, c), 1)
    eye = (ri == ci).astype(F32)
    ltri16 = (ri >= ci).astype(BF16)
    gc_chunks = []
    for ic in range(nc):
        g_c = g_all[ic * c:(ic + 1) * c, :]
        g_hi = g_c.astype(BF16)
        r1 = g_c - g_hi.astype(F32)
        g_mid = r1.astype(BF16)
        g_lo = (r1 - g_mid.astype(F32)).astype(BF16)
        parts = _dot(ltri16, jnp.concatenate([g_hi, g_mid, g_lo], axis=1))
        gc_c = parts[:, 0:LANES] + parts[:, LANES:2 * LANES] + parts[:, 2 * LANES:3 * LANES]
        gc_chunks.append(gc_c)
        gct_ref[ic] = gc_c.T
        bt_ref[ic] = beta_all[ic * c:(ic + 1) * c, :].T
    gc_all = jnp.concatenate(gc_chunks, axis=0)
    for h in heads:
        gcb_ref[h] = jnp.broadcast_to(gc_all[:, n_heads + h:n_heads + h + 1], (t, LANES))
        bb_ref[h] = jnp.broadcast_to(beta_all[:, h:h + 1], (t, LANES))

    def intra(ic, carry):
        rows = pl.ds(pl.multiple_of(ic * c, c), c)
        causal = ri >= ci
        blk = ri ^ ci
        a_mats, t_mats = [], []
        for h in heads:
            cols = slice(h * LANES, (h + 1) * LANES)
            k16 = k_ref[0, rows, cols].astype(BF16)
            q16 = q_ref[0, rows, cols].astype(BF16)
            qkk = _dot_nt(jnp.concatenate([q16, k16], axis=0), k16)
            gcr = jnp.broadcast_to(gct_ref[ic, n_heads + h:n_heads + h + 1, :], (c, c))
            gamma = jnp.exp(jnp.where(causal, gcb_ref[h, rows, :] - gcr, -1e30))
            ak_ref[h, ic, 0:c, :] = (qkk[0:c] * gamma).astype(BF16)
            a_mat = jnp.where(ri > ci, bb_ref[h, rows, :] * qkk[c:2 * c] * gamma, 0.0)
            a_mats.append(a_mat)
            t_mats.append(eye - jnp.where(blk == 1, a_mat, 0.0))
        s = 2
        while s < c:
            join = (blk >= s) & (blk < 2 * s)
            t16 = [tm.astype(BF16) for tm in t_mats]
            ys = [_dot(t16[h], jnp.where(join, a_mats[h], 0.0).astype(BF16)) for h in heads]
            t_mats = [t_mats[h] - _dot(ys[h].astype(BF16), t16[h]) for h in heads]
            s *= 2
        for h in heads:
            cols = slice(h * LANES, (h + 1) * LANES)
            gcb = gcb_ref[h, rows, :]
            kh = k_ref[0, rows, cols]
            e_col = jnp.exp(gcb)
            br = jnp.broadcast_to(bt_ref[ic, h:h + 1, :], (c, c))
            rhs = jnp.concatenate([v_ref[0, rows, cols].astype(BF16), (kh * e_col).astype(BF16)], axis=1)
            uw = _dot((t_mats[h] * br).astype(BF16), rhs)
            u_ref[rows, cols] = uw[:, 0:LANES]
            wq_ref[h, ic, 0:c, :] = uw[:, LANES:2 * LANES].astype(BF16)
            wq_ref[h, ic, c:2 * c, :] = (q_ref[0, rows, cols] * e_col).astype(BF16)
            g_last = gcb[c - 1:c, :]
            ak_ref[h, ic, c:2 * c, :] = (kh * jnp.exp(g_last - gcb)).T.astype(BF16)
            dec_ref[ic, h:h + 1, :] = jnp.exp(g_last)
        return carry

    lax.fori_loop(0, nc, intra, 0)

    def inter(ic, carry):
        rows = pl.ds(pl.multiple_of(ic * c, c), c)
        sts, qss, vns = [], [], []
        for h in heads:
            cols = slice(h * LANES, (h + 1) * LANES)
            st = state_ref[h]
            wq = _dot(wq_ref[h, ic], st.astype(BF16))
            sts.append(st)
            vns.append((u_ref[rows, cols] - wq[0:c]).astype(BF16))
            qss.append(wq[c:2 * c])
        for h in heads:
            cols = slice(h * LANES, (h + 1) * LANES)
            ak = _dot(ak_ref[h, ic], vns[h])
            os_ref[rows, cols] = qss[h] + ak[0:c]
            state_ref[h] = sts[h] * dec_ref[ic, h:h + 1, :] + ak[c:2 * c]
        return carry

    lax.fori_loop(0, nc, inter, 0)

    for h in heads:
        cols = slice(h * LANES, (h + 1) * LANES)
        o_ref[0, :, cols] = (_rms(os_ref[:, cols], dn_ref[...]) * _silu(z_ref[0, :, cols])).astype(BF16)


def delta_rule(q, k, v, p3, alog_pad, dt_pad, dn_norm, *, n_heads, t, z_blk, lg_blk):
    b, s, width = q.shape
    c = DN_CHUNK
    nc = t // c
    assert width == n_heads * LANES and c == LANES and t % c == 0
    qspec = pl.BlockSpec((1, t, width), lambda bi, ti: (bi, ti, 0))
    row = pl.BlockSpec((1, LANES), lambda bi, ti: (0, 0))
    return pl.pallas_call(
        functools.partial(_delta_kernel, t=t, n_heads=n_heads),
        out_shape=jax.ShapeDtypeStruct((b, s, width), BF16),
        grid=(b, s // t),
        in_specs=[qspec, qspec, qspec,
                  pl.BlockSpec((1, t, LANES), lambda bi, ti: (bi, ti, lg_blk)),
                  pl.BlockSpec((1, t, width), lambda bi, ti: (bi, ti, z_blk)),
                  row, row, row],
        out_specs=qspec,
        scratch_shapes=[pltpu.VMEM((n_heads, LANES, LANES), F32),
                        pltpu.VMEM((n_heads, t, LANES), F32),
                        pltpu.VMEM((n_heads, t, LANES), F32),
                        pltpu.VMEM((nc, LANES, c), F32),
                        pltpu.VMEM((nc, LANES, c), F32),
                        pltpu.VMEM((t, width), F32),
                        pltpu.VMEM((n_heads, nc, 2 * c, LANES), BF16),
                        pltpu.VMEM((n_heads, nc, 2 * c, LANES), BF16),
                        pltpu.VMEM((nc, n_heads, LANES), F32),
                        pltpu.VMEM((t, width), F32)],
        compiler_params=_cparams("parallel", "arbitrary"),
        name="delta_rule",
    )(q, k, v, p3, p3, alog_pad, dt_pad, dn_norm)


def _pool_kernel(xp_ref, w_ref, sc_ref, o_ref, buf_ref, *, t, halo, gdim):
    ti = pl.program_id(1)

    @pl.when(ti == 0)
    def _():
        buf_ref[0:halo, :] = jnp.zeros((halo, buf_ref.shape[1]), F32)

    buf_ref[halo:halo + t, :] = xp_ref[0]
    pos = ti * t + lax.broadcasted_iota(jnp.int32, (t, 1), 0) + 1
    for gi, win in enumerate(POOL_WINDOWS):
        cols = slice(gi * gdim, (gi + 1) * gdim)
        acc = buf_ref[halo:halo + t, cols]
        tok = acc
        for d in range(1, win):
            acc = acc + buf_ref[halo - d:halo - d + t, cols]
        pooled = acc / jnp.minimum(pos, win).astype(F32) - tok
        y = _dot(pooled.astype(BF16), w_ref[gi])
        o_ref[0, :, cols] = (y * sc_ref[:, cols]).astype(BF16)
    buf_ref[0:halo, :] = buf_ref[t:t + halo, :]


def pool_mixer(p3, w_grp, scale, *, t, xp_blk):
    b, s, _ = p3.shape
    groups, gdim, _ = w_grp.shape
    width = groups * gdim
    halo = 16
    return pl.pallas_call(
        functools.partial(_pool_kernel, t=t, halo=halo, gdim=gdim),
        out_shape=jax.ShapeDtypeStruct((b, s, width), BF16),
        grid=(b, s // t),
        in_specs=[pl.BlockSpec((1, t, width), lambda bi, ti: (bi, ti, xp_blk)),
                  pl.BlockSpec((groups, gdim, gdim), lambda bi, ti: (0, 0, 0)),
                  pl.BlockSpec((1, width), lambda bi, ti: (0, 0))],
        out_specs=pl.BlockSpec((1, t, width), lambda bi, ti: (bi, ti, 0)),
        scratch_shapes=[pltpu.VMEM((t + halo, width), F32)],
        compiler_params=_cparams("parallel", "arbitrary"),
        name="pool_mixer",
    )(p3, w_grp, scale)


def _conf_conv_kernel(u_ref, w_ref, b_ref, g_ref, beta_ref, o_ref, buf_ref, y_ref, *, t, halo, kw):
    width = u_ref.shape[2]
    ncg = width // LANES
    half = t // 2

    @pl.when(pl.program_id(1) == 0)
    def _():
        buf_ref[:, 0:halo, :] = jnp.zeros((ncg, halo, LANES), F32)

    rsum = jnp.zeros((t, LANES), F32)
    for cg in range(ncg):
        cols = slice(cg * LANES, (cg + 1) * LANES)
        buf_ref[cg, halo:halo + t, :] = u_ref[0, :, cols]
        bias = jnp.broadcast_to(b_ref[:, cols], (half, LANES))
        acc_e, acc_o = bias, bias
        for j in range(kw):
            r0 = halo - (kw - 1) + j
            wj = w_ref[j:j + 1, cols]
            acc_e = acc_e + wj * buf_ref[cg, pl.ds(r0, half, stride=2), :]
            acc_o = acc_o + wj * buf_ref[cg, pl.ds(r0 + 1, half, stride=2), :]
        y_ref[cg, pl.ds(0, half, stride=2), :] = acc_e
        y_ref[cg, pl.ds(1, half, stride=2), :] = acc_o
        buf_ref[cg, 0:halo, :] = buf_ref[cg, t:t + halo, :]
        rsum = rsum + y_ref[cg]
    mu = jnp.sum(rsum, axis=-1, keepdims=True) * (1.0 / width)
    vsum = jnp.zeros((t, LANES), F32)
    for cg in range(ncg):
        yc = y_ref[cg] - mu
        vsum = vsum + yc * yc
    rstd = lax.rsqrt(jnp.sum(vsum, axis=-1, keepdims=True) * (1.0 / width) + EPS)
    for cg in range(ncg):
        cols = slice(cg * LANES, (cg + 1) * LANES)
        yn = (y_ref[cg] - mu) * rstd * g_ref[:, cols] + beta_ref[:, cols]
        o_ref[0, :, cols] = _silu(yn).astype(BF16)


def conf_conv(u3, dw, dw_b, ln_g, ln_b, *, t):
    b, s, width = u3.shape
    kw = dw.shape[0]
    halo = 32
    assert kw - 1 <= halo and t % 16 == 0
    row = pl.BlockSpec((1, width), lambda bi, ti: (0, 0))
    blk = pl.BlockSpec((1, t, width), lambda bi, ti: (bi, ti, 0))
    return pl.pallas_call(
        functools.partial(_conf_conv_kernel, t=t, halo=halo, kw=kw),
        out_shape=jax.ShapeDtypeStruct((b, s, width), BF16),
        grid=(b, s // t),
        in_specs=[blk, pl.BlockSpec((kw, width), lambda bi, ti: (0, 0)), row, row, row],
        out_specs=blk,
        scratch_shapes=[pltpu.VMEM((width // LANES, t + halo, LANES), F32),
                        pltpu.VMEM((width // LANES, t, LANES), F32)],
        compiler_params=_cparams("parallel", "arbitrary"),
        name="conf_conv",
    )(u3, dw, dw_b, ln_g, ln_b)


def _even_layer(x2, b, s, g_pre, g_post, w_in, conv_w, a_log, dt_bias, dn_norm, pool_w, pool_scale, w_out):
    d = x2.shape[1]
    n_heads = a_log.shape[0]
    dn_w = n_heads * dn_norm.shape[0]
    pool_width = pool_scale.shape[0]
    o2 = 4 * dn_w
    o4 = o2 + 2 * n_heads
    tn = 1792
    n_pad = tn * pl.cdiv(4 * dn_w + pool_width + LANES, tn)
    w_cat = jnp.concatenate(
        [w_in[:, :o2], w_in[:, o4:], w_in[:, o2:o4],
         jnp.zeros((d, n_pad - w_in.shape[1]), w_in.dtype)], axis=1).astype(BF16)
    p = norm_proj(x2, g_pre.reshape(1, d), w_cat, tm=1024, tn=tn)
    p3 = p.reshape(b, s, n_pad)
    q, k, v = qkv_prep(p3, conv_w, n_heads=n_heads, t=512)
    pad = jnp.zeros((n_heads,), F32)
    tail = jnp.zeros((LANES - 2 * n_heads,), F32)
    alog_pad = jnp.concatenate([pad, a_log, tail]).reshape(1, LANES)
    dt_pad = jnp.concatenate([pad, dt_bias, tail]).reshape(1, LANES)
    o = delta_rule(q, k, v, p3, alog_pad, dt_pad, dn_norm.reshape(1, LANES), n_heads=n_heads, t=512,
                   z_blk=3, lg_blk=(4 * dn_w + pool_width) // LANES)
    y_pool = pool_mixer(p3, pool_w.astype(BF16), pool_scale.reshape(1, pool_width), t=512,
                        xp_blk=4 * dn_w // pool_width)
    w_out16 = w_out.astype(BF16)
    return out_proj([o.reshape(b * s, dn_w), y_pool.reshape(b * s, pool_width)],
                    [w_out16[:dn_w], w_out16[dn_w:]], x2, g_post.reshape(1, d), tm=512)


def _odd_layer(x2, b, s, g_pre, g_post, w_in, dw, dw_b, ln_g, ln_b, w_out):
    d = x2.shape[1]
    width = w_out.shape[0]
    u = norm_glu(x2, g_pre.reshape(1, d), w_in.astype(BF16), tm=1024, tn=1024)
    a = conf_conv(u.reshape(b, s, width), dw, dw_b.reshape(1, width), ln_g.reshape(1, width),
                  ln_b.reshape(1, width), t=256)
    return out_proj([a.reshape(b * s, width)], [w_out.astype(BF16)], x2, g_post.reshape(1, d), tm=512)


def kernel(x, norm_mix_pre, norm_mix_post, norm_mlp_pre, norm_mlp_post, even_w_in, even_conv, even_a_log, even_dt_bias, even_dn_norm, even_pool_w, even_pool_scale, even_w_out, odd_w_in, odd_dw, odd_dw_b, odd_ln_g, odd_ln_b, odd_w_out, mlp_w_up, mlp_w_down):
    b, s, d = x.shape
    depth = norm_mix_pre.shape[0]
    x2 = x.reshape(b * s, d)
    w_up16 = mlp_w_up.astype(BF16)
    w_down16 = mlp_w_down.astype(BF16)
    for i in range(depth):
        j = i // 2
        if i % 2 == 0:
            x2 = _even_layer(x2, b, s, norm_mix_pre[i], norm_mix_post[i], even_w_in[j], even_conv[j],
                             even_a_log[j], even_dt_bias[j], even_dn_norm[j], even_pool_w[j],
                             even_pool_scale[j], even_w_out[j])
        else:
            x2 = _odd_layer(x2, b, s, norm_mix_pre[i], norm_mix_post[i], odd_w_in[j], odd_dw[j],
                            odd_dw_b[j], odd_ln_g[j], odd_ln_b[j], odd_w_out[j])
        x2 = mlp(x2, norm_mlp_pre[i].reshape(1, d), w_up16, w_down16, norm_mlp_post[i].reshape(1, d),
                 layer=i, tm=512, tf=1024)
    return x2.reshape(b, s, d)
```

```python
import functools

import jax
import jax.numpy as jnp
from jax import lax
from jax.experimental import pallas as pl
from jax.experimental.pallas import tpu as pltpu

F32 = jnp.float32
BF16 = jnp.bfloat16
EPS = 1e-6
LANES = 128
DN_CHUNK = 128
SHORT_CONV = 4
POOL_WINDOWS = (2, 4, 8, 16)
VMEM_LIMIT = 56 * 1024 * 1024

TM_PROJ = 1024
TN_EVEN_PROJ = 1792
TN_GLU = 1024
TM_OUT = 512
TM_MLP, TF_MLP = 512, 1024
T_QKV = T_DELTA = T_POOL = 512
T_CONV = 256


def _cparams(*sem):
    return pltpu.CompilerParams(dimension_semantics=sem, vmem_limit_bytes=VMEM_LIMIT)


def _dot(a, b, precision=None):
    return jnp.dot(a, b, preferred_element_type=F32, precision=precision)


def _dot_nt(a, b, precision=None):
    return lax.dot_general(a, b, (((1,), (1,)), ((), ())), preferred_element_type=F32, precision=precision)


def _rms(x, g):
    ms = jnp.mean(x * x, axis=-1, keepdims=True)
    return x * lax.rsqrt(ms + EPS) * g


def _silu(x):
    return x * jax.nn.sigmoid(x)


def _normed(x_ref, g_ref, xn_ref, body):
    j = pl.program_id(1)

    @pl.when(j == 0)
    def _():
        xn = _rms(x_ref[...], g_ref[...]).astype(BF16)
        xn_ref[...] = xn
        body(xn)

    @pl.when(j > 0)
    def _():
        body(xn_ref[...])


def _norm_proj_kernel(x_ref, g_ref, w_ref, o_ref, xn_ref):
    def body(xn):
        o_ref[...] = _dot(xn, w_ref[...])

    _normed(x_ref, g_ref, xn_ref, body)


def _norm_glu_kernel(x_ref, g_ref, wa_ref, wg_ref, o_ref, xn_ref):
    def body(xn):
        a = _dot(xn, wa_ref[...])
        gate = _dot(xn, wg_ref[...])
        o_ref[...] = a * jax.nn.sigmoid(gate)

    _normed(x_ref, g_ref, xn_ref, body)


def norm_proj(x, g, w, *, tm, tn):
    m, d = x.shape
    n = w.shape[1]
    return pl.pallas_call(
        _norm_proj_kernel,
        out_shape=jax.ShapeDtypeStruct((m, n), F32),
        grid=(m // tm, n // tn),
        in_specs=[pl.BlockSpec((tm, d), lambda i, j: (i, 0)),
                  pl.BlockSpec((1, d), lambda i, j: (0, 0)),
                  pl.BlockSpec((d, tn), lambda i, j: (0, j))],
        out_specs=pl.BlockSpec((tm, tn), lambda i, j: (i, j)),
        scratch_shapes=[pltpu.VMEM((tm, d), BF16)],
        compiler_params=_cparams("parallel", "arbitrary"),
        name="norm_proj",
    )(x, g, w)


def norm_glu(x, g, w, *, tm, tn):
    m, d = x.shape
    n = w.shape[1] // 2
    nj = n // tn
    return pl.pallas_call(
        _norm_glu_kernel,
        out_shape=jax.ShapeDtypeStruct((m, n), F32),
        grid=(m // tm, nj),
        in_specs=[pl.BlockSpec((tm, d), lambda i, j: (i, 0)),
                  pl.BlockSpec((1, d), lambda i, j: (0, 0)),
                  pl.BlockSpec((d, tn), lambda i, j: (0, j)),
                  pl.BlockSpec((d, tn), lambda i, j: (0, j + nj))],
        out_specs=pl.BlockSpec((tm, tn), lambda i, j: (i, j)),
        scratch_shapes=[pltpu.VMEM((tm, d), BF16)],
        compiler_params=_cparams("parallel", "arbitrary"),
        name="norm_glu",
    )(x, g, w, w)


def _out_proj_kernel(*refs, n_in):
    a_refs = refs[:n_in]
    w_refs = refs[n_in:2 * n_in]
    x_ref, g_ref, o_ref = refs[2 * n_in:]
    mix = _dot(a_refs[0][...], w_refs[0][...])
    for a_ref, w_ref in zip(a_refs[1:], w_refs[1:]):
        mix = mix + _dot(a_ref[...], w_ref[...])
    o_ref[...] = x_ref[...] + _rms(mix, g_ref[...])


def out_proj(acts, ws, x, g, *, tm):
    m, d = x.shape
    n_in = len(acts)
    in_specs = ([pl.BlockSpec((tm, a.shape[1]), lambda i: (i, 0)) for a in acts]
                + [pl.BlockSpec(w.shape, lambda i: (0, 0)) for w in ws]
                + [pl.BlockSpec((tm, d), lambda i: (i, 0)), pl.BlockSpec((1, d), lambda i: (0, 0))])
    return pl.pallas_call(
        functools.partial(_out_proj_kernel, n_in=n_in),
        out_shape=jax.ShapeDtypeStruct((m, d), F32),
        grid=(m // tm,),
        in_specs=in_specs,
        out_specs=pl.BlockSpec((tm, d), lambda i: (i, 0)),
        compiler_params=_cparams("parallel"),
        name="out_proj",
    )(*acts, *ws, x, g)


def _mlp_kernel(x_ref, gpre_ref, wup_ref, wdn_ref, gpost_ref, o_ref, xn_ref, acc_ref):
    f = pl.program_id(1)
    last = pl.num_programs(1) - 1

    def ff_part(xn):
        h = _dot(xn, wup_ref[...])
        h = jnp.square(jnp.maximum(h, 0.0)).astype(BF16)
        return _dot(h, wdn_ref[...])

    @pl.when(f == 0)
    def _():
        xn = _rms(x_ref[...], gpre_ref[...]).astype(BF16)
        xn_ref[...] = xn
        acc_ref[...] = ff_part(xn)

    @pl.when((f > 0) & (f < last))
    def _():
        acc_ref[...] += ff_part(xn_ref[...])

    @pl.when(f == last)
    def _():
        o_ref[...] = x_ref[...] + _rms(acc_ref[...] + ff_part(xn_ref[...]), gpost_ref[...])


def mlp(x, gpre, wup, wdn, gpost, *, layer, tm, tf):
    m, d = x.shape
    ff = wup.shape[2]
    assert ff // tf >= 2
    return pl.pallas_call(
        _mlp_kernel,
        out_shape=jax.ShapeDtypeStruct((m, d), F32),
        grid=(m // tm, ff // tf),
        in_specs=[pl.BlockSpec((tm, d), lambda i, f: (i, 0)),
                  pl.BlockSpec((1, d), lambda i, f: (0, 0)),
                  pl.BlockSpec((None, d, tf), lambda i, f: (layer, 0, f)),
                  pl.BlockSpec((None, tf, d), lambda i, f: (layer, f, 0)),
                  pl.BlockSpec((1, d), lambda i, f: (0, 0))],
        out_specs=pl.BlockSpec((tm, d), lambda i, f: (i, 0)),
        scratch_shapes=[pltpu.VMEM((tm, d), BF16), pltpu.VMEM((tm, d), F32)],
        compiler_params=_cparams("parallel", "arbitrary"),
        name="mlp",
    )(x, gpre, wup, wdn, gpost)


def _qkv_prep_kernel(p_ref, cw_ref, q_ref, k_ref, v_ref, buf_ref, y_ref, *, t, n_heads, halo):
    ncg = 3 * n_heads
    half = t // 2

    @pl.when(pl.program_id(1) == 0)
    def _():
        buf_ref[:, 0:halo, :] = jnp.zeros((ncg, halo, LANES), F32)

    outs = (q_ref, k_ref, v_ref)
    for cg in range(ncg):
        cols = slice(cg * LANES, (cg + 1) * LANES)
        which, hh = divmod(cg, n_heads)
        buf_ref[cg, halo:halo + t, :] = p_ref[0, :, cols]
        for par in range(2):
            acc = None
            for j in range(SHORT_CONV):
                r0 = halo - (SHORT_CONV - 1) + j + par
                term = cw_ref[j:j + 1, cols] * buf_ref[cg, pl.ds(r0, half, stride=2), :]
                acc = term if acc is None else acc + term
            y = _silu(acc)
            if which < 2:
                y = y * lax.rsqrt(jnp.sum(y * y, axis=-1, keepdims=True) + EPS)
            if which == 0:
                y = y * (LANES ** -0.5)
            y_ref[cg, pl.ds(par, half, stride=2), :] = y
        buf_ref[cg, 0:halo, :] = buf_ref[cg, t:t + halo, :]
        outs[which][0, :, hh * LANES:(hh + 1) * LANES] = y_ref[cg]


def qkv_prep(p3, conv_w, *, n_heads, t):
    b, s, _ = p3.shape
    width = n_heads * LANES
    halo = 8
    out = jax.ShapeDtypeStruct((b, s, width), F32)
    ospec = pl.BlockSpec((1, t, width), lambda bi, ti: (bi, ti, 0))
    return pl.pallas_call(
        functools.partial(_qkv_prep_kernel, t=t, n_heads=n_heads, halo=halo),
        out_shape=(out, out, out),
        grid=(b, s // t),
        in_specs=[pl.BlockSpec((1, t, 3 * width), lambda bi, ti: (bi, ti, 0)),
                  pl.BlockSpec((SHORT_CONV, 3 * width), lambda bi, ti: (0, 0))],
        out_specs=(ospec, ospec, ospec),
        scratch_shapes=[pltpu.VMEM((3 * n_heads, t + halo, LANES), F32),
                        pltpu.VMEM((3 * n_heads, t, LANES), F32)],
        compiler_params=_cparams("parallel", "arbitrary"),
        name="qkv_prep",
    )(p3, conv_w)


def _delta_kernel(q_ref, k_ref, v_ref, lg_ref, z_ref, alog_ref, dt_ref, dn_ref, o_ref,
                  state_ref, gcb_ref, bb_ref, gct_ref, bt_ref, u_ref, wq_ref, ak_ref, dec_ref, os_ref,
                  *, t, n_heads):
    c = DN_CHUNK
    nc = t // c
    heads = range(n_heads)

    @pl.when(pl.program_id(1) == 0)
    def _():
        state_ref[...] = jnp.zeros_like(state_ref)

    lg = lg_ref[0]
    beta_all = jax.nn.sigmoid(lg)
    sp_in = lg + dt_ref[...]
    softplus = jnp.maximum(sp_in, 0.0) + jnp.log1p(jnp.exp(-jnp.abs(sp_in)))
    g_all = -jnp.exp(alog_ref[...]) * softplus
    ri = lax.broadcasted_iota(jnp.int32, (c, c), 0)
    ci = lax.broadcasted_iota(jnp.int32, (c, c), 1)
    eye = (ri == ci).astype(F32)
    ltri16 = (ri >= ci).astype(BF16)
    gc_chunks = []
    for ic in range(nc):
        g_c = g_all[ic * c:(ic + 1) * c, :]
        g_hi = g_c.astype(BF16)
        r1 = g_c - g_hi.astype(F32)
        g_mid = r1.astype(BF16)
        g_lo = (r1 - g_mid.astype(F32)).astype(BF16)
        parts = _dot(ltri16, jnp.concatenate([g_hi, g_mid, g_lo], axis=1))
        gc_c = parts[:, 0:LANES] + parts[:, LANES:2 * LANES] + parts[:, 2 * LANES:3 * LANES]
        gc_chunks.append(gc_c)
        gct_ref[ic] = gc_c.T
        bt_ref[ic] = beta_all[ic * c:(ic + 1) * c, :].T
    gc_all = jnp.concatenate(gc_chunks, axis=0)
    for h in heads:
        gcb_ref[h] = jnp.broadcast_to(gc_all[:, n_heads + h:n_heads + h + 1], (t, LANES))
        bb_ref[h] = jnp.broadcast_to(beta_all[:, h:h + 1], (t, LANES))

    cpi = 2 if nc % 2 == 0 else 1

    def intra(it, carry):
        causal = ri >= ci
        blk = ri ^ ci
        units = [(it * cpi + sub, h) for sub in range(cpi) for h in heads]
        nu = range(len(units))
        a_mats, t_mats = [], []
        for ic, h in units:
            rows = pl.ds(pl.multiple_of(ic * c, c), c)
            cols = slice(h * LANES, (h + 1) * LANES)
            k16 = k_ref[0, rows, cols].astype(BF16)
            q16 = q_ref[0, rows, cols].astype(BF16)
            qkk = _dot_nt(jnp.concatenate([q16, k16], axis=0), k16)
            gcr = jnp.broadcast_to(gct_ref[ic, n_heads + h:n_heads + h + 1, :], (c, c))
            gamma = jnp.exp(jnp.where(causal, gcb_ref[h, rows, :] - gcr, -1e30))
            ak_ref[h, ic, 0:c, :] = (qkk[0:c] * gamma).astype(BF16)
            a_mat = jnp.where(ri > ci, bb_ref[h, rows, :] * qkk[c:2 * c] * gamma, 0.0)
            a_mats.append(a_mat)
            t_mats.append(eye - jnp.where(blk == 1, a_mat, 0.0))
        s = 2
        while s < c:
            join = (blk >= s) & (blk < 2 * s)
            t16 = [tm.astype(BF16) for tm in t_mats]
            ys = [_dot(t16[n], jnp.where(join, a_mats[n], 0.0).astype(BF16)) for n in nu]
            t_mats = [t_mats[n] - _dot(ys[n].astype(BF16), t16[n]) for n in nu]
            s *= 2
        for n, (ic, h) in enumerate(units):
            rows = pl.ds(pl.multiple_of(ic * c, c), c)
            cols = slice(h * LANES, (h + 1) * LANES)
            gcb = gcb_ref[h, rows, :]
            kh = k_ref[0, rows, cols]
            e_col = jnp.exp(gcb)
            br = jnp.broadcast_to(bt_ref[ic, h:h + 1, :], (c, c))
            rhs = jnp.concatenate([v_ref[0, rows, cols].astype(BF16), (kh * e_col).astype(BF16)], axis=1)
            uw = _dot((t_mats[n] * br).astype(BF16), rhs)
            u_ref[rows, cols] = uw[:, 0:LANES]
            wq_ref[h, ic, 0:c, :] = uw[:, LANES:2 * LANES].astype(BF16)
            wq_ref[h, ic, c:2 * c, :] = (q_ref[0, rows, cols] * e_col).astype(BF16)
            g_last = gcb[c - 1:c, :]
            ak_ref[h, ic, c:2 * c, :] = (kh * jnp.exp(g_last - gcb)).T.astype(BF16)
            dec_ref[ic, h:h + 1, :] = jnp.exp(g_last)
        return carry

    lax.fori_loop(0, nc // cpi, intra, 0)

    def inter(ic, carry):
        rows = pl.ds(pl.multiple_of(ic * c, c), c)
        sts, qss, vns = [], [], []
        for h in heads:
            cols = slice(h * LANES, (h + 1) * LANES)
            st = state_ref[h]
            wq = _dot(wq_ref[h, ic], st.astype(BF16))
            sts.append(st)
            vns.append((u_ref[rows, cols] - wq[0:c]).astype(BF16))
            qss.append(wq[c:2 * c])
        for h in heads:
            cols = slice(h * LANES, (h + 1) * LANES)
            ak = _dot(ak_ref[h, ic], vns[h])
            os_ref[rows, cols] = qss[h] + ak[0:c]
            state_ref[h] = sts[h] * dec_ref[ic, h:h + 1, :] + ak[c:2 * c]
        return carry

    lax.fori_loop(0, nc, inter, 0)

    for h in heads:
        cols = slice(h * LANES, (h + 1) * LANES)
        o_ref[0, :, cols] = (_rms(os_ref[:, cols], dn_ref[...]) * _silu(z_ref[0, :, cols])).astype(BF16)


def delta_rule(q, k, v, p3, alog_pad, dt_pad, dn_norm, *, n_heads, t, z_blk, lg_blk):
    b, s, width = q.shape
    c = DN_CHUNK
    nc = t // c
    assert width == n_heads * LANES and c == LANES and t % c == 0
    qspec = pl.BlockSpec((1, t, width), lambda bi, ti: (bi, ti, 0))
    row = pl.BlockSpec((1, LANES), lambda bi, ti: (0, 0))
    return pl.pallas_call(
        functools.partial(_delta_kernel, t=t, n_heads=n_heads),
        out_shape=jax.ShapeDtypeStruct((b, s, width), BF16),
        grid=(b, s // t),
        in_specs=[qspec, qspec, qspec,
                  pl.BlockSpec((1, t, LANES), lambda bi, ti: (bi, ti, lg_blk)),
                  pl.BlockSpec((1, t, width), lambda bi, ti: (bi, ti, z_blk)),
                  row, row, row],
        out_specs=qspec,
        scratch_shapes=[pltpu.VMEM((n_heads, LANES, LANES), F32),
                        pltpu.VMEM((n_heads, t, LANES), F32),
                        pltpu.VMEM((n_heads, t, LANES), F32),
                        pltpu.VMEM((nc, LANES, c), F32),
                        pltpu.VMEM((nc, LANES, c), F32),
                        pltpu.VMEM((t, width), F32),
                        pltpu.VMEM((n_heads, nc, 2 * c, LANES), BF16),
                        pltpu.VMEM((n_heads, nc, 2 * c, LANES), BF16),
                        pltpu.VMEM((nc, n_heads, LANES), F32),
                        pltpu.VMEM((t, width), F32)],
        compiler_params=_cparams("parallel", "arbitrary"),
        name="delta_rule",
    )(q, k, v, p3, p3, alog_pad, dt_pad, dn_norm)


def _pool_kernel(xp_ref, w_ref, sc_ref, o_ref, buf_ref, *, t, halo, gdim):
    ti = pl.program_id(1)

    @pl.when(ti == 0)
    def _():
        buf_ref[0:halo, :] = jnp.zeros((halo, buf_ref.shape[1]), F32)

    buf_ref[halo:halo + t, :] = xp_ref[0]
    pos = ti * t + lax.broadcasted_iota(jnp.int32, (t, 1), 0) + 1
    for gi, win in enumerate(POOL_WINDOWS):
        cols = slice(gi * gdim, (gi + 1) * gdim)
        acc = buf_ref[halo:halo + t, cols]
        tok = acc
        for d in range(1, win):
            acc = acc + buf_ref[halo - d:halo - d + t, cols]
        pooled = acc / jnp.minimum(pos, win).astype(F32) - tok
        y = _dot(pooled.astype(BF16), w_ref[gi])
        o_ref[0, :, cols] = (y * sc_ref[:, cols]).astype(BF16)
    buf_ref[0:halo, :] = buf_ref[t:t + halo, :]


def pool_mixer(p3, w_grp, scale, *, t, xp_blk):
    b, s, _ = p3.shape
    groups, gdim, _ = w_grp.shape
    width = groups * gdim
    halo = 16
    return pl.pallas_call(
        functools.partial(_pool_kernel, t=t, halo=halo, gdim=gdim),
        out_shape=jax.ShapeDtypeStruct((b, s, width), BF16),
        grid=(b, s // t),
        in_specs=[pl.BlockSpec((1, t, width), lambda bi, ti: (bi, ti, xp_blk)),
                  pl.BlockSpec((groups, gdim, gdim), lambda bi, ti: (0, 0, 0)),
                  pl.BlockSpec((1, width), lambda bi, ti: (0, 0))],
        out_specs=pl.BlockSpec((1, t, width), lambda bi, ti: (bi, ti, 0)),
        scratch_shapes=[pltpu.VMEM((t + halo, width), F32)],
        compiler_params=_cparams("parallel", "arbitrary"),
        name="pool_mixer",
    )(p3, w_grp, scale)


def _conf_conv_kernel(u_ref, w_ref, b_ref, g_ref, beta_ref, o_ref, buf_ref, y_ref, *, t, halo, kw):
    width = u_ref.shape[2]
    ncg = width // LANES
    half = t // 2

    @pl.when(pl.program_id(1) == 0)
    def _():
        buf_ref[:, 0:halo, :] = jnp.zeros((ncg, halo, LANES), F32)

    rsum = jnp.zeros((t, LANES), F32)
    for cg in range(ncg):
        cols = slice(cg * LANES, (cg + 1) * LANES)
        buf_ref[cg, halo:halo + t, :] = u_ref[0, :, cols]
        bias = jnp.broadcast_to(b_ref[:, cols], (half, LANES))
        acc_e, acc_o = bias, bias
        for j in range(kw):
            r0 = halo - (kw - 1) + j
            wj = w_ref[j:j + 1, cols]
            acc_e = acc_e + wj * buf_ref[cg, pl.ds(r0, half, stride=2), :]
            acc_o = acc_o + wj * buf_ref[cg, pl.ds(r0 + 1, half, stride=2), :]
        y_ref[cg, pl.ds(0, half, stride=2), :] = acc_e
        y_ref[cg, pl.ds(1, half, stride=2), :] = acc_o
        buf_ref[cg, 0:halo, :] = buf_ref[cg, t:t + halo, :]
        rsum = rsum + y_ref[cg]
    mu = jnp.sum(rsum, axis=-1, keepdims=True) * (1.0 / width)
    vsum = jnp.zeros((t, LANES), F32)
    for cg in range(ncg):
        yc = y_ref[cg] - mu
        vsum = vsum + yc * yc
    rstd = lax.rsqrt(jnp.sum(vsum, axis=-1, keepdims=True) * (1.0 / width) + EPS)
    for cg in range(ncg):
        cols = slice(cg * LANES, (cg + 1) * LANES)
        yn = (y_ref[cg] - mu) * rstd * g_ref[:, cols] + beta_ref[:, cols]
        o_ref[0, :, cols] = _silu(yn).astype(BF16)


def conf_conv(u3, dw, dw_b, ln_g, ln_b, *, t):
    b, s, width = u3.shape
    kw = dw.shape[0]
    halo = 32
    assert kw - 1 <= halo and t % 16 == 0
    row = pl.BlockSpec((1, width), lambda bi, ti: (0, 0))
    blk = pl.BlockSpec((1, t, width), lambda bi, ti: (bi, ti, 0))
    return pl.pallas_call(
        functools.partial(_conf_conv_kernel, t=t, halo=halo, kw=kw),
        out_shape=jax.ShapeDtypeStruct((b, s, width), BF16),
        grid=(b, s // t),
        in_specs=[blk, pl.BlockSpec((kw, width), lambda bi, ti: (0, 0)), row, row, row],
        out_specs=blk,
        scratch_shapes=[pltpu.VMEM((width // LANES, t + halo, LANES), F32),
                        pltpu.VMEM((width // LANES, t, LANES), F32)],
        compiler_params=_cparams("parallel", "arbitrary"),
        name="conf_conv",
    )(u3, dw, dw_b, ln_g, ln_b)


def _even_layer(x2, b, s, g_pre, g_post, w_in, conv_w, a_log, dt_bias, dn_norm, pool_w, pool_scale, w_out):
    d = x2.shape[1]
    n_heads = a_log.shape[0]
    dn_w = n_heads * dn_norm.shape[0]
    pool_width = pool_scale.shape[0]
    o2 = 4 * dn_w
    o4 = o2 + 2 * n_heads
    tn = TN_EVEN_PROJ
    n_pad = tn * pl.cdiv(4 * dn_w + pool_width + LANES, tn)
    w_cat = jnp.concatenate(
        [w_in[:, :o2], w_in[:, o4:], w_in[:, o2:o4],
         jnp.zeros((d, n_pad - w_in.shape[1]), w_in.dtype)], axis=1).astype(BF16)
    p = norm_proj(x2, g_pre.reshape(1, d), w_cat, tm=TM_PROJ, tn=tn)
    p3 = p.reshape(b, s, n_pad)
    q, k, v = qkv_prep(p3, conv_w, n_heads=n_heads, t=T_QKV)
    pad = jnp.zeros((n_heads,), F32)
    tail = jnp.zeros((LANES - 2 * n_heads,), F32)
    alog_pad = jnp.concatenate([pad, a_log, tail]).reshape(1, LANES)
    dt_pad = jnp.concatenate([pad, dt_bias, tail]).reshape(1, LANES)
    o = delta_rule(q, k, v, p3, alog_pad, dt_pad, dn_norm.reshape(1, LANES), n_heads=n_heads, t=T_DELTA,
                   z_blk=3, lg_blk=(4 * dn_w + pool_width) // LANES)
    y_pool = pool_mixer(p3, pool_w.astype(BF16), pool_scale.reshape(1, pool_width), t=T_POOL,
                        xp_blk=4 * dn_w // pool_width)
    w_out16 = w_out.astype(BF16)
    return out_proj([o.reshape(b * s, dn_w), y_pool.reshape(b * s, pool_width)],
                    [w_out16[:dn_w], w_out16[dn_w:]], x2, g_post.reshape(1, d), tm=TM_OUT)


def _odd_layer(x2, b, s, g_pre, g_post, w_in, dw, dw_b, ln_g, ln_b, w_out):
    d = x2.shape[1]
    width = w_out.shape[0]
    u = norm_glu(x2, g_pre.reshape(1, d), w_in.astype(BF16), tm=TM_PROJ, tn=TN_GLU)
    a = conf_conv(u.reshape(b, s, width), dw, dw_b.reshape(1, width), ln_g.reshape(1, width),
                  ln_b.reshape(1, width), t=T_CONV)
    return out_proj([a.reshape(b * s, width)], [w_out.astype(BF16)], x2, g_post.reshape(1, d), tm=TM_OUT)


def kernel(x, norm_mix_pre, norm_mix_post, norm_mlp_pre, norm_mlp_post, even_w_in, even_conv, even_a_log, even_dt_bias, even_dn_norm, even_pool_w, even_pool_scale, even_w_out, odd_w_in, odd_dw, odd_dw_b, odd_ln_g, odd_ln_b, odd_w_out, mlp_w_up, mlp_w_down):
    b, s, d = x.shape
    depth = norm_mix_pre.shape[0]
    x2 = x.reshape(b * s, d)
    w_up16 = mlp_w_up.astype(BF16)
    w_down16 = mlp_w_down.astype(BF16)
    for i in range(depth):
        j = i // 2
        if i % 2 == 0:
            x2 = _even_layer(x2, b, s, norm_mix_pre[i], norm_mix_post[i], even_w_in[j], even_conv[j],
                             even_a_log[j], even_dt_bias[j], even_dn_norm[j], even_pool_w[j],
                             even_pool_scale[j], even_w_out[j])
        else:
            x2 = _odd_layer(x2, b, s, norm_mix_pre[i], norm_mix_post[i], odd_w_in[j], odd_dw[j],
                            odd_dw_b[j], odd_ln_g[j], odd_ln_b[j], odd_w_out[j])
        x2 = mlp(x2, norm_mlp_pre[i].reshape(1, d), w_up16, w_down16, norm_mlp_post[i].reshape(1, d),
                 layer=i, tm=TM_MLP, tf=TF_MLP)
    return x2.reshape(b, s, d)
```

```python
import functools

import jax
import jax.numpy as jnp
from jax import lax
from jax.experimental import pallas as pl
from jax.experimental.pallas import tpu as pltpu

F32 = jnp.float32
BF16 = jnp.bfloat16
EPS = 1e-6
LANES = 128
DN_CHUNK = 128
SHORT_CONV = 4
POOL_WINDOWS = (2, 4, 8, 16)
VMEM_LIMIT = 56 * 1024 * 1024

TM_PROJ = 1024
TN_EVEN_PROJ = 1792
TN_GLU = 1024
TM_OUT = 512
TM_MLP, TF_MLP = 512, 1024
T_QKV = T_DELTA = T_POOL = 512
T_CONV = 256


def _cparams(*sem):
    return pltpu.CompilerParams(dimension_semantics=sem, vmem_limit_bytes=VMEM_LIMIT)


def _dot(a, b, precision=None):
    return jnp.dot(a, b, preferred_element_type=F32, precision=precision)


def _dot_nt(a, b, precision=None):
    return lax.dot_general(a, b, (((1,), (1,)), ((), ())), preferred_element_type=F32, precision=precision)


def _rms(x, g):
    ms = jnp.mean(x * x, axis=-1, keepdims=True)
    return x * lax.rsqrt(ms + EPS) * g


def _silu(x):
    return x * jax.nn.sigmoid(x)


def _normed(x_ref, g_ref, xn_ref, body):
    j = pl.program_id(1)

    @pl.when(j == 0)
    def _():
        xn = _rms(x_ref[...], g_ref[...]).astype(BF16)
        xn_ref[...] = xn
        body(xn)

    @pl.when(j > 0)
    def _():
        body(xn_ref[...])


def _norm_proj_kernel(x_ref, g_ref, w_ref, o_ref, xn_ref):
    def body(xn):
        o_ref[...] = _dot(xn, w_ref[...])

    _normed(x_ref, g_ref, xn_ref, body)


def _norm_glu_kernel(x_ref, g_ref, wa_ref, wg_ref, o_ref, xn_ref):
    def body(xn):
        a = _dot(xn, wa_ref[...])
        gate = _dot(xn, wg_ref[...])
        o_ref[...] = a * jax.nn.sigmoid(gate)

    _normed(x_ref, g_ref, xn_ref, body)


def norm_proj(x, g, w, *, tm, tn):
    m, d = x.shape
    n = w.shape[1]
    return pl.pallas_call(
        _norm_proj_kernel,
        out_shape=jax.ShapeDtypeStruct((m, n), F32),
        grid=(m // tm, n // tn),
        in_specs=[pl.BlockSpec((tm, d), lambda i, j: (i, 0)),
                  pl.BlockSpec((1, d), lambda i, j: (0, 0)),
                  pl.BlockSpec((d, tn), lambda i, j: (0, j))],
        out_specs=pl.BlockSpec((tm, tn), lambda i, j: (i, j)),
        scratch_shapes=[pltpu.VMEM((tm, d), BF16)],
        compiler_params=_cparams("parallel", "arbitrary"),
        name="norm_proj",
    )(x, g, w)


def norm_glu(x, g, w, *, tm, tn):
    m, d = x.shape
    n = w.shape[1] // 2
    nj = n // tn
    return pl.pallas_call(
        _norm_glu_kernel,
        out_shape=jax.ShapeDtypeStruct((m, n), F32),
        grid=(m // tm, nj),
        in_specs=[pl.BlockSpec((tm, d), lambda i, j: (i, 0)),
                  pl.BlockSpec((1, d), lambda i, j: (0, 0)),
                  pl.BlockSpec((d, tn), lambda i, j: (0, j)),
                  pl.BlockSpec((d, tn), lambda i, j: (0, j + nj))],
        out_specs=pl.BlockSpec((tm, tn), lambda i, j: (i, j)),
        scratch_shapes=[pltpu.VMEM((tm, d), BF16)],
        compiler_params=_cparams("parallel", "arbitrary"),
        name="norm_glu",
    )(x, g, w, w)


def _out_proj_kernel(*refs, n_in):
    a_refs = refs[:n_in]
    w_refs = refs[n_in:2 * n_in]
    x_ref, g_ref, o_ref = refs[2 * n_in:]
    mix = _dot(a_refs[0][...], w_refs[0][...])
    for a_ref, w_ref in zip(a_refs[1:], w_refs[1:]):
        mix = mix + _dot(a_ref[...], w_ref[...])
    o_ref[...] = x_ref[...] + _rms(mix, g_ref[...])


def out_proj(acts, ws, x, g, *, tm):
    m, d = x.shape
    n_in = len(acts)
    in_specs = ([pl.BlockSpec((tm, a.shape[1]), lambda i: (i, 0)) for a in acts]
                + [pl.BlockSpec(w.shape, lambda i: (0, 0)) for w in ws]
                + [pl.BlockSpec((tm, d), lambda i: (i, 0)), pl.BlockSpec((1, d), lambda i: (0, 0))])
    return pl.pallas_call(
        functools.partial(_out_proj_kernel, n_in=n_in),
        out_shape=jax.ShapeDtypeStruct((m, d), F32),
        grid=(m // tm,),
        in_specs=in_specs,
        out_specs=pl.BlockSpec((tm, d), lambda i: (i, 0)),
        compiler_params=_cparams("parallel"),
        name="out_proj",
    )(*acts, *ws, x, g)


def _mlp_kernel(x_ref, gpre_ref, wup_ref, wdn_ref, gpost_ref, o_ref, xn_ref, acc_ref):
    f = pl.program_id(1)
    last = pl.num_programs(1) - 1

    def ff_part(xn):
        h = _dot(xn, wup_ref[...])
        h = jnp.square(jnp.maximum(h, 0.0)).astype(BF16)
        return _dot(h, wdn_ref[...])

    @pl.when(f == 0)
    def _():
        xn = _rms(x_ref[...], gpre_ref[...]).astype(BF16)
        xn_ref[...] = xn
        acc_ref[...] = ff_part(xn)

    @pl.when((f > 0) & (f < last))
    def _():
        acc_ref[...] += ff_part(xn_ref[...])

    @pl.when(f == last)
    def _():
        o_ref[...] = x_ref[...] + _rms(acc_ref[...] + ff_part(xn_ref[...]), gpost_ref[...])


def mlp(x, gpre, wup, wdn, gpost, *, layer, tm, tf):
    m, d = x.shape
    ff = wup.shape[2]
    assert ff // tf >= 2
    return pl.pallas_call(
        _mlp_kernel,
        out_shape=jax.ShapeDtypeStruct((m, d), F32),
        grid=(m // tm, ff // tf),
        in_specs=[pl.BlockSpec((tm, d), lambda i, f: (i, 0)),
                  pl.BlockSpec((1, d), lambda i, f: (0, 0)),
                  pl.BlockSpec((None, d, tf), lambda i, f: (layer, 0, f)),
                  pl.BlockSpec((None, tf, d), lambda i, f: (layer, f, 0)),
                  pl.BlockSpec((1, d), lambda i, f: (0, 0))],
        out_specs=pl.BlockSpec((tm, d), lambda i, f: (i, 0)),
        scratch_shapes=[pltpu.VMEM((tm, d), BF16), pltpu.VMEM((tm, d), F32)],
        compiler_params=_cparams("parallel", "arbitrary"),
        name="mlp",
    )(x, gpre, wup, wdn, gpost)


def _qkv_prep_kernel(p_ref, cw_ref, q_ref, k_ref, v_ref, buf_ref, y_ref, *, t, n_heads, halo):
    ncg = 3 * n_heads
    half = t // 2

    @pl.when(pl.program_id(1) == 0)
    def _():
        buf_ref[:, 0:halo, :] = jnp.zeros((ncg, halo, LANES), F32)

    outs = (q_ref, k_ref, v_ref)
    for cg in range(ncg):
        cols = slice(cg * LANES, (cg + 1) * LANES)
        which, hh = divmod(cg, n_heads)
        buf_ref[cg, halo:halo + t, :] = p_ref[0, :, cols]
        for par in range(2):
            acc = None
            for j in range(SHORT_CONV):
                r0 = halo - (SHORT_CONV - 1) + j + par
                term = cw_ref[j:j + 1, cols] * buf_ref[cg, pl.ds(r0, half, stride=2), :]
                acc = term if acc is None else acc + term
            y = _silu(acc)
            if which < 2:
                y = y * lax.rsqrt(jnp.sum(y * y, axis=-1, keepdims=True) + EPS)
            if which == 0:
                y = y * (LANES ** -0.5)
            y_ref[cg, pl.ds(par, half, stride=2), :] = y
        buf_ref[cg, 0:halo, :] = buf_ref[cg, t:t + halo, :]
        outs[which][0, :, hh * LANES:(hh + 1) * LANES] = y_ref[cg]


def qkv_prep(p3, conv_w, *, n_heads, t):
    b, s, _ = p3.shape
    width = n_heads * LANES
    halo = 8
    out = jax.ShapeDtypeStruct((b, s, width), F32)
    ospec = pl.BlockSpec((1, t, width), lambda bi, ti: (bi, ti, 0))
    return pl.pallas_call(
        functools.partial(_qkv_prep_kernel, t=t, n_heads=n_heads, halo=halo),
        out_shape=(out, out, out),
        grid=(b, s // t),
        in_specs=[pl.BlockSpec((1, t, 3 * width), lambda bi, ti: (bi, ti, 0)),
                  pl.BlockSpec((SHORT_CONV, 3 * width), lambda bi, ti: (0, 0))],
        out_specs=(ospec, ospec, ospec),
        scratch_shapes=[pltpu.VMEM((3 * n_heads, t + halo, LANES), F32),
                        pltpu.VMEM((3 * n_heads, t, LANES), F32)],
        compiler_params=_cparams("parallel", "arbitrary"),
        name="qkv_prep",
    )(p3, conv_w)


def _delta_kernel(q_ref, k_ref, v_ref, lg_ref, z_ref, alog_ref, dt_ref, dn_ref, o_ref,
                  state_ref, gcb_ref, bb_ref, gct_ref, bt_ref, u_ref, wq_ref, ak_ref, dec_ref, os_ref,
                  *, t, n_heads):
    c = DN_CHUNK
    nc = t // c
    heads = range(n_heads)

    @pl.when(pl.program_id(1) == 0)
    def _():
        state_ref[...] = jnp.zeros_like(state_ref)

    lg = lg_ref[0]
    beta_all = jax.nn.sigmoid(lg)
    sp_in = lg + dt_ref[...]
    softplus = jnp.maximum(sp_in, 0.0) + jnp.log1p(jnp.exp(-jnp.abs(sp_in)))
    g_all = -jnp.exp(alog_ref[...]) * softplus
    ri = lax.broadcasted_iota(jnp.int32, (c, c), 0)
    ci = lax.broadcasted_iota(jnp.int32, (c, c), 1)
    eye = (ri == ci).astype(F32)
    ltri16 = (ri >= ci).astype(BF16)
    gc_chunks = []
    for ic in range(nc):
        g_c = g_all[ic * c:(ic + 1) * c, :]
        g_hi = g_c.astype(BF16)
        r1 = g_c - g_hi.astype(F32)
        g_mid = r1.astype(BF16)
        g_lo = (r1 - g_mid.astype(F32)).astype(BF16)
        parts = _dot(ltri16, jnp.concatenate([g_hi, g_mid, g_lo], axis=1))
        gc_c = parts[:, 0:LANES] + parts[:, LANES:2 * LANES] + parts[:, 2 * LANES:3 * LANES]
        gc_chunks.append(gc_c)
        gct_ref[ic] = gc_c.T
        bt_ref[ic] = beta_all[ic * c:(ic + 1) * c, :].T
    gc_all = jnp.concatenate(gc_chunks, axis=0)
    for h in heads:
        gcb_ref[h] = jnp.broadcast_to(gc_all[:, n_heads + h:n_heads + h + 1], (t, LANES))
        bb_ref[h] = jnp.broadcast_to(beta_all[:, h:h + 1], (t, LANES))

    cpi = next(n for n in (4, 2, 1) if nc % n == 0)

    def intra(it, carry):
        causal = ri >= ci
        blk = ri ^ ci
        units = [(it * cpi + sub, h) for sub in range(cpi) for h in heads]
        nu = range(len(units))
        a_mats, t_mats = [], []
        for ic, h in units:
            rows = pl.ds(pl.multiple_of(ic * c, c), c)
            cols = slice(h * LANES, (h + 1) * LANES)
            k16 = k_ref[0, rows, cols].astype(BF16)
            q16 = q_ref[0, rows, cols].astype(BF16)
            qkk = _dot_nt(jnp.concatenate([q16, k16], axis=0), k16)
            gcr = jnp.broadcast_to(gct_ref[ic, n_heads + h:n_heads + h + 1, :], (c, c))
            gamma = jnp.exp(jnp.where(causal, gcb_ref[h, rows, :] - gcr, -1e30))
            ak_ref[h, ic, 0:c, :] = (qkk[0:c] * gamma).astype(BF16)
            a_mat = jnp.where(ri > ci, bb_ref[h, rows, :] * qkk[c:2 * c] * gamma, 0.0)
            a_mats.append(a_mat)
            t_mats.append(eye - jnp.where(blk == 1, a_mat, 0.0))
        s = 2
        while s < c:
            join = (blk >= s) & (blk < 2 * s)
            t16 = [tm.astype(BF16) for tm in t_mats]
            ys = [_dot(t16[n], jnp.where(join, a_mats[n], 0.0).astype(BF16)) for n in nu]
            t_mats = [t_mats[n] - _dot(ys[n].astype(BF16), t16[n]) for n in nu]
            s *= 2
        for n, (ic, h) in enumerate(units):
            rows = pl.ds(pl.multiple_of(ic * c, c), c)
            cols = slice(h * LANES, (h + 1) * LANES)
            gcb = gcb_ref[h, rows, :]
            kh = k_ref[0, rows, cols]
            e_col = jnp.exp(gcb)
            br = jnp.broadcast_to(bt_ref[ic, h:h + 1, :], (c, c))
            rhs = jnp.concatenate([v_ref[0, rows, cols].astype(BF16), (kh * e_col).astype(BF16)], axis=1)
            uw = _dot((t_mats[n] * br).astype(BF16), rhs)
            u_ref[rows, cols] = uw[:, 0:LANES]
            wq_ref[h, ic, 0:c, :] = uw[:, LANES:2 * LANES].astype(BF16)
            wq_ref[h, ic, c:2 * c, :] = (q_ref[0, rows, cols] * e_col).astype(BF16)
            g_last = gcb[c - 1:c, :]
            ak_ref[h, ic, c:2 * c, :] = (kh * jnp.exp(g_last - gcb)).T.astype(BF16)
            dec_ref[ic, h:h + 1, :] = jnp.exp(g_last)
        return carry

    lax.fori_loop(0, nc // cpi, intra, 0)

    def inter(ic, carry):
        rows = pl.ds(pl.multiple_of(ic * c, c), c)
        sts, qss, vns = [], [], []
        for h in heads:
            cols = slice(h * LANES, (h + 1) * LANES)
            st = state_ref[h]
            wq = _dot(wq_ref[h, ic], st.astype(BF16))
            sts.append(st)
            vns.append((u_ref[rows, cols] - wq[0:c]).astype(BF16))
            qss.append(wq[c:2 * c])
        for h in heads:
            cols = slice(h * LANES, (h + 1) * LANES)
            ak = _dot(ak_ref[h, ic], vns[h])
            os_ref[rows, cols] = qss[h] + ak[0:c]
            state_ref[h] = sts[h] * dec_ref[ic, h:h + 1, :] + ak[c:2 * c]
        return carry

    lax.fori_loop(0, nc, inter, 0)

    for h in heads:
        cols = slice(h * LANES, (h + 1) * LANES)
        o_ref[0, :, cols] = (_rms(os_ref[:, cols], dn_ref[...]) * _silu(z_ref[0, :, cols])).astype(BF16)


def delta_rule(q, k, v, p3, alog_pad, dt_pad, dn_norm, *, n_heads, t, z_blk, lg_blk):
    b, s, width = q.shape
    c = DN_CHUNK
    nc = t // c
    assert width == n_heads * LANES and c == LANES and t % c == 0
    qspec = pl.BlockSpec((1, t, width), lambda bi, ti: (bi, ti, 0))
    row = pl.BlockSpec((1, LANES), lambda bi, ti: (0, 0))
    return pl.pallas_call(
        functools.partial(_delta_kernel, t=t, n_heads=n_heads),
        out_shape=jax.ShapeDtypeStruct((b, s, width), BF16),
        grid=(b, s // t),
        in_specs=[qspec, qspec, qspec,
                  pl.BlockSpec((1, t, LANES), lambda bi, ti: (bi, ti, lg_blk)),
                  pl.BlockSpec((1, t, width), lambda bi, ti: (bi, ti, z_blk)),
                  row, row, row],
        out_specs=qspec,
        scratch_shapes=[pltpu.VMEM((n_heads, LANES, LANES), F32),
                        pltpu.VMEM((n_heads, t, LANES), F32),
                        pltpu.VMEM((n_heads, t, LANES), F32),
                        pltpu.VMEM((nc, LANES, c), F32),
                        pltpu.VMEM((nc, LANES, c), F32),
                        pltpu.VMEM((t, width), F32),
                        pltpu.VMEM((n_heads, nc, 2 * c, LANES), BF16),
                        pltpu.VMEM((n_heads, nc, 2 * c, LANES), BF16),
                        pltpu.VMEM((nc, n_heads, LANES), F32),
                        pltpu.VMEM((t, width), F32)],
        compiler_params=_cparams("parallel", "arbitrary"),
        name="delta_rule",
    )(q, k, v, p3, p3, alog_pad, dt_pad, dn_norm)


def _pool_kernel(xp_ref, w_ref, sc_ref, o_ref, buf_ref, *, t, halo, gdim):
    ti = pl.program_id(1)

    @pl.when(ti == 0)
    def _():
        buf_ref[0:halo, :] = jnp.zeros((halo, buf_ref.shape[1]), F32)

    buf_ref[halo:halo + t, :] = xp_ref[0]
    pos = ti * t + lax.broadcasted_iota(jnp.int32, (t, 1), 0) + 1
    for gi, win in enumerate(POOL_WINDOWS):
        cols = slice(gi * gdim, (gi + 1) * gdim)
        acc = buf_ref[halo:halo + t, cols]
        tok = acc
        for d in range(1, win):
            acc = acc + buf_ref[halo - d:halo - d + t, cols]
        pooled = acc / jnp.minimum(pos, win).astype(F32) - tok
        y = _dot(pooled.astype(BF16), w_ref[gi])
        o_ref[0, :, cols] = (y * sc_ref[:, cols]).astype(BF16)
    buf_ref[0:halo, :] = buf_ref[t:t + halo, :]


def pool_mixer(p3, w_grp, scale, *, t, xp_blk):
    b, s, _ = p3.shape
    groups, gdim, _ = w_grp.shape
    width = groups * gdim
    halo = 16
    return pl.pallas_call(
        functools.partial(_pool_kernel, t=t, halo=halo, gdim=gdim),
        out_shape=jax.ShapeDtypeStruct((b, s, width), BF16),
        grid=(b, s // t),
        in_specs=[pl.BlockSpec((1, t, width), lambda bi, ti: (bi, ti, xp_blk)),
                  pl.BlockSpec((groups, gdim, gdim), lambda bi, ti: (0, 0, 0)),
                  pl.BlockSpec((1, width), lambda bi, ti: (0, 0))],
        out_specs=pl.BlockSpec((1, t, width), lambda bi, ti: (bi, ti, 0)),
        scratch_shapes=[pltpu.VMEM((t + halo, width), F32)],
        compiler_params=_cparams("parallel", "arbitrary"),
        name="pool_mixer",
    )(p3, w_grp, scale)


def _conf_conv_kernel(u_ref, w_ref, b_ref, g_ref, beta_ref, o_ref, buf_ref, y_ref, *, t, halo, kw):
    width = u_ref.shape[2]
    ncg = width // LANES
    half = t // 2

    @pl.when(pl.program_id(1) == 0)
    def _():
        buf_ref[:, 0:halo, :] = jnp.zeros((ncg, halo, LANES), F32)

    rsum = jnp.zeros((t, LANES), F32)
    for cg in range(ncg):
        cols = slice(cg * LANES, (cg + 1) * LANES)
        buf_ref[cg, halo:halo + t, :] = u_ref[0, :, cols]
        bias = jnp.broadcast_to(b_ref[:, cols], (half, LANES))
        acc_e, acc_o = bias, bias
        for j in range(kw):
            r0 = halo - (kw - 1) + j
            wj = w_ref[j:j + 1, cols]
            acc_e = acc_e + wj * buf_ref[cg, pl.ds(r0, half, stride=2), :]
            acc_o = acc_o + wj * buf_ref[cg, pl.ds(r0 + 1, half, stride=2), :]
        y_ref[cg, pl.ds(0, half, stride=2), :] = acc_e
        y_ref[cg, pl.ds(1, half, stride=2), :] = acc_o
        buf_ref[cg, 0:halo, :] = buf_ref[cg, t:t + halo, :]
        rsum = rsum + y_ref[cg]
    mu = jnp.sum(rsum, axis=-1, keepdims=True) * (1.0 / width)
    vsum = jnp.zeros((t, LANES), F32)
    for cg in range(ncg):
        yc = y_ref[cg] - mu
        vsum = vsum + yc * yc
    rstd = lax.rsqrt(jnp.sum(vsum, axis=-1, keepdims=True) * (1.0 / width) + EPS)
    for cg in range(ncg):
        cols = slice(cg * LANES, (cg + 1) * LANES)
        yn = (y_ref[cg] - mu) * rstd * g_ref[:, cols] + beta_ref[:, cols]
        o_ref[0, :, cols] = _silu(yn).astype(BF16)


def conf_conv(u3, dw, dw_b, ln_g, ln_b, *, t):
    b, s, width = u3.shape
    kw = dw.shape[0]
    halo = 32
    assert kw - 1 <= halo and t % 16 == 0
    row = pl.BlockSpec((1, width), lambda bi, ti: (0, 0))
    blk = pl.BlockSpec((1, t, width), lambda bi, ti: (bi, ti, 0))
    return pl.pallas_call(
        functools.partial(_conf_conv_kernel, t=t, halo=halo, kw=kw),
        out_shape=jax.ShapeDtypeStruct((b, s, width), BF16),
        grid=(b, s // t),
        in_specs=[blk, pl.BlockSpec((kw, width), lambda bi, ti: (0, 0)), row, row, row],
        out_specs=blk,
        scratch_shapes=[pltpu.VMEM((width // LANES, t + halo, LANES), F32),
                        pltpu.VMEM((width // LANES, t, LANES), F32)],
        compiler_params=_cparams("parallel", "arbitrary"),
        name="conf_conv",
    )(u3, dw, dw_b, ln_g, ln_b)


def _even_layer(x2, b, s, g_pre, g_post, w_in, conv_w, a_log, dt_bias, dn_norm, pool_w, pool_scale, w_out):
    d = x2.shape[1]
    n_heads = a_log.shape[0]
    dn_w = n_heads * dn_norm.shape[0]
    pool_width = pool_scale.shape[0]
    o2 = 4 * dn_w
    o4 = o2 + 2 * n_heads
    tn = TN_EVEN_PROJ
    n_pad = tn * pl.cdiv(4 * dn_w + pool_width + LANES, tn)
    w_cat = jnp.concatenate(
        [w_in[:, :o2], w_in[:, o4:], w_in[:, o2:o4],
         jnp.zeros((d, n_pad - w_in.shape[1]), w_in.dtype)], axis=1).astype(BF16)
    p = norm_proj(x2, g_pre.reshape(1, d), w_cat, tm=TM_PROJ, tn=tn)
    p3 = p.reshape(b, s, n_pad)
    q, k, v = qkv_prep(p3, conv_w, n_heads=n_heads, t=T_QKV)
    pad = jnp.zeros((n_heads,), F32)
    tail = jnp.zeros((LANES - 2 * n_heads,), F32)
    alog_pad = jnp.concatenate([pad, a_log, tail]).reshape(1, LANES)
    dt_pad = jnp.concatenate([pad, dt_bias, tail]).reshape(1, LANES)
    o = delta_rule(q, k, v, p3, alog_pad, dt_pad, dn_norm.reshape(1, LANES), n_heads=n_heads, t=T_DELTA,
                   z_blk=3, lg_blk=(4 * dn_w + pool_width) // LANES)
    y_pool = pool_mixer(p3, pool_w.astype(BF16), pool_scale.reshape(1, pool_width), t=T_POOL,
                        xp_blk=4 * dn_w // pool_width)
    w_out16 = w_out.astype(BF16)
    return out_proj([o.reshape(b * s, dn_w), y_pool.reshape(b * s, pool_width)],
                    [w_out16[:dn_w], w_out16[dn_w:]], x2, g_post.reshape(1, d), tm=TM_OUT)


def _odd_layer(x2, b, s, g_pre, g_post, w_in, dw, dw_b, ln_g, ln_b, w_out):
    d = x2.shape[1]
    width = w_out.shape[0]
    u = norm_glu(x2, g_pre.reshape(1, d), w_in.astype(BF16), tm=TM_PROJ, tn=TN_GLU)
    a = conf_conv(u.reshape(b, s, width), dw, dw_b.reshape(1, width), ln_g.reshape(1, width),
                  ln_b.reshape(1, width), t=T_CONV)
    return out_proj([a.reshape(b * s, width)], [w_out.astype(BF16)], x2, g_post.reshape(1, d), tm=TM_OUT)


def kernel(x, norm_mix_pre, norm_mix_post, norm_mlp_pre, norm_mlp_post, even_w_in, even_conv, even_a_log, even_dt_bias, even_dn_norm, even_pool_w, even_pool_scale, even_w_out, odd_w_in, odd_dw, odd_dw_b, odd_ln_g, odd_ln_b, odd_w_out, mlp_w_up, mlp_w_down):
    b, s, d = x.shape
    depth = norm_mix_pre.shape[0]
    x2 = x.reshape(b * s, d)
    w_up16 = mlp_w_up.astype(BF16)
    w_down16 = mlp_w_down.astype(BF16)
    for i in range(depth):
        j = i // 2
        if i % 2 == 0:
            x2 = _even_layer(x2, b, s, norm_mix_pre[i], norm_mix_post[i], even_w_in[j], even_conv[j],
                             even_a_log[j], even_dt_bias[j], even_dn_norm[j], even_pool_w[j],
                             even_pool_scale[j], even_w_out[j])
        else:
            x2 = _odd_layer(x2, b, s, norm_mix_pre[i], norm_mix_post[i], odd_w_in[j], odd_dw[j],
                            odd_dw_b[j], odd_ln_g[j], odd_ln_b[j], odd_w_out[j])
        x2 = mlp(x2, norm_mlp_pre[i].reshape(1, d), w_up16, w_down16, norm_mlp_post[i].reshape(1, d),
                 layer=i, tm=TM_MLP, tf=TF_MLP)
    return x2.reshape(b, s, d)
```

```python
import functools

import jax
import jax.numpy as jnp
from jax import lax
from jax.experimental import pallas as pl
from jax.experimental.pallas import tpu as pltpu

F32 = jnp.float32
BF16 = jnp.bfloat16
EPS = 1e-6
LANES = 128
DN_CHUNK = 128
SHORT_CONV = 4
POOL_WINDOWS = (2, 4, 8, 16)
VMEM_LIMIT = 56 * 1024 * 1024

TM_PROJ = 1024
TN_EVEN_PROJ = 1792
TN_GLU = 1024
TM_OUT = 512
TM_MLP, TF_MLP = 512, 1024
T_QKV = T_DELTA = T_POOL = 512
T_CONV = 256


def _cparams(*sem):
    return pltpu.CompilerParams(dimension_semantics=sem, vmem_limit_bytes=VMEM_LIMIT)


def _dot(a, b, precision=None):
    return jnp.dot(a, b, preferred_element_type=F32, precision=precision)


def _dot_nt(a, b, precision=None):
    return lax.dot_general(a, b, (((1,), (1,)), ((), ())), preferred_element_type=F32, precision=precision)


def _rms(x, g):
    ms = jnp.mean(x * x, axis=-1, keepdims=True)
    return x * lax.rsqrt(ms + EPS) * g


def _silu(x):
    return x * jax.nn.sigmoid(x)


def _normed(x_ref, g_ref, xn_ref, body):
    j = pl.program_id(1)

    @pl.when(j == 0)
    def _():
        xn = _rms(x_ref[...], g_ref[...]).astype(BF16)
        xn_ref[...] = xn
        body(xn)

    @pl.when(j > 0)
    def _():
        body(xn_ref[...])


def _norm_proj_kernel(x_ref, g_ref, w_ref, o_ref, xn_ref):
    def body(xn):
        o_ref[...] = _dot(xn, w_ref[...])

    _normed(x_ref, g_ref, xn_ref, body)


def _norm_glu_kernel(x_ref, g_ref, wa_ref, wg_ref, o_ref, xn_ref):
    def body(xn):
        a = _dot(xn, wa_ref[...])
        gate = _dot(xn, wg_ref[...])
        o_ref[...] = a * jax.nn.sigmoid(gate)

    _normed(x_ref, g_ref, xn_ref, body)


def norm_proj(x, g, w, *, tm, tn):
    m, d = x.shape
    n = w.shape[1]
    return pl.pallas_call(
        _norm_proj_kernel,
        out_shape=jax.ShapeDtypeStruct((m, n), F32),
        grid=(m // tm, n // tn),
        in_specs=[pl.BlockSpec((tm, d), lambda i, j: (i, 0)),
                  pl.BlockSpec((1, d), lambda i, j: (0, 0)),
                  pl.BlockSpec((d, tn), lambda i, j: (0, j))],
        out_specs=pl.BlockSpec((tm, tn), lambda i, j: (i, j)),
        scratch_shapes=[pltpu.VMEM((tm, d), BF16)],
        compiler_params=_cparams("parallel", "arbitrary"),
        name="norm_proj",
    )(x, g, w)


def norm_glu(x, g, w, *, tm, tn):
    m, d = x.shape
    n = w.shape[1] // 2
    nj = n // tn
    return pl.pallas_call(
        _norm_glu_kernel,
        out_shape=jax.ShapeDtypeStruct((m, n), F32),
        grid=(m // tm, nj),
        in_specs=[pl.BlockSpec((tm, d), lambda i, j: (i, 0)),
                  pl.BlockSpec((1, d), lambda i, j: (0, 0)),
                  pl.BlockSpec((d, tn), lambda i, j: (0, j)),
                  pl.BlockSpec((d, tn), lambda i, j: (0, j + nj))],
        out_specs=pl.BlockSpec((tm, tn), lambda i, j: (i, j)),
        scratch_shapes=[pltpu.VMEM((tm, d), BF16)],
        compiler_params=_cparams("parallel", "arbitrary"),
        name="norm_glu",
    )(x, g, w, w)


def _out_proj_kernel(*refs, n_in):
    a_refs = refs[:n_in]
    w_refs = refs[n_in:2 * n_in]
    x_ref, g_ref, o_ref = refs[2 * n_in:]
    mix = _dot(a_refs[0][...], w_refs[0][...])
    for a_ref, w_ref in zip(a_refs[1:], w_refs[1:]):
        mix = mix + _dot(a_ref[...], w_ref[...])
    o_ref[...] = x_ref[...] + _rms(mix, g_ref[...])


def out_proj(acts, ws, x, g, *, tm):
    m, d = x.shape
    n_in = len(acts)
    in_specs = ([pl.BlockSpec((tm, a.shape[1]), lambda i: (i, 0)) for a in acts]
                + [pl.BlockSpec(w.shape, lambda i: (0, 0)) for w in ws]
                + [pl.BlockSpec((tm, d), lambda i: (i, 0)), pl.BlockSpec((1, d), lambda i: (0, 0))])
    return pl.pallas_call(
        functools.partial(_out_proj_kernel, n_in=n_in),
        out_shape=jax.ShapeDtypeStruct((m, d), F32),
        grid=(m // tm,),
        in_specs=in_specs,
        out_specs=pl.BlockSpec((tm, d), lambda i: (i, 0)),
        compiler_params=_cparams("parallel"),
        name="out_proj",
    )(*acts, *ws, x, g)


def _mlp_kernel(x_ref, gpre_ref, wup_ref, wdn_ref, gpost_ref, o_ref, xn_ref, acc_ref):
    f = pl.program_id(1)
    last = pl.num_programs(1) - 1

    def ff_part(xn):
        h = _dot(xn, wup_ref[...])
        h = jnp.square(jnp.maximum(h, 0.0)).astype(BF16)
        return _dot(h, wdn_ref[...])

    @pl.when(f == 0)
    def _():
        xn = _rms(x_ref[...], gpre_ref[...]).astype(BF16)
        xn_ref[...] = xn
        acc_ref[...] = ff_part(xn)

    @pl.when((f > 0) & (f < last))
    def _():
        acc_ref[...] += ff_part(xn_ref[...])

    @pl.when(f == last)
    def _():
        o_ref[...] = x_ref[...] + _rms(acc_ref[...] + ff_part(xn_ref[...]), gpost_ref[...])


def mlp(x, gpre, wup, wdn, gpost, *, layer, tm, tf):
    m, d = x.shape
    ff = wup.shape[2]
    assert ff // tf >= 2
    return pl.pallas_call(
        _mlp_kernel,
        out_shape=jax.ShapeDtypeStruct((m, d), F32),
        grid=(m // tm, ff // tf),
        in_specs=[pl.BlockSpec((tm, d), lambda i, f: (i, 0)),
                  pl.BlockSpec((1, d), lambda i, f: (0, 0)),
                  pl.BlockSpec((None, d, tf), lambda i, f: (layer, 0, f)),
                  pl.BlockSpec((None, tf, d), lambda i, f: (layer, f, 0)),
                  pl.BlockSpec((1, d), lambda i, f: (0, 0))],
        out_specs=pl.BlockSpec((tm, d), lambda i, f: (i, 0)),
        scratch_shapes=[pltpu.VMEM((tm, d), BF16), pltpu.VMEM((tm, d), F32)],
        compiler_params=_cparams("parallel", "arbitrary"),
        name="mlp",
    )(x, gpre, wup, wdn, gpost)


def _qkv_prep_kernel(p_ref, cw_ref, q_ref, k_ref, v_ref, buf_ref, y_ref, *, t, n_heads, halo):
    ncg = 3 * n_heads
    half = t // 2

    @pl.when(pl.program_id(1) == 0)
    def _():
        buf_ref[:, 0:halo, :] = jnp.zeros((ncg, halo, LANES), F32)

    outs = (q_ref, k_ref, v_ref)
    for cg in range(ncg):
        cols = slice(cg * LANES, (cg + 1) * LANES)
        which, hh = divmod(cg, n_heads)
        buf_ref[cg, halo:halo + t, :] = p_ref[0, :, cols]
        for par in range(2):
            acc = None
            for j in range(SHORT_CONV):
                r0 = halo - (SHORT_CONV - 1) + j + par
                term = cw_ref[j:j + 1, cols] * buf_ref[cg, pl.ds(r0, half, stride=2), :]
                acc = term if acc is None else acc + term
            y = _silu(acc)
            if which < 2:
                y = y * lax.rsqrt(jnp.sum(y * y, axis=-1, keepdims=True) + EPS)
            if which == 0:
                y = y * (LANES ** -0.5)
            y_ref[cg, pl.ds(par, half, stride=2), :] = y
        buf_ref[cg, 0:halo, :] = buf_ref[cg, t:t + halo, :]
        outs[which][0, :, hh * LANES:(hh + 1) * LANES] = y_ref[cg]


def qkv_prep(p3, conv_w, *, n_heads, t):
    b, s, _ = p3.shape
    width = n_heads * LANES
    halo = 8
    out = jax.ShapeDtypeStruct((b, s, width), F32)
    ospec = pl.BlockSpec((1, t, width), lambda bi, ti: (bi, ti, 0))
    return pl.pallas_call(
        functools.partial(_qkv_prep_kernel, t=t, n_heads=n_heads, halo=halo),
        out_shape=(out, out, out),
        grid=(b, s // t),
        in_specs=[pl.BlockSpec((1, t, 3 * width), lambda bi, ti: (bi, ti, 0)),
                  pl.BlockSpec((SHORT_CONV, 3 * width), lambda bi, ti: (0, 0))],
        out_specs=(ospec, ospec, ospec),
        scratch_shapes=[pltpu.VMEM((3 * n_heads, t + halo, LANES), F32),
                        pltpu.VMEM((3 * n_heads, t, LANES), F32)],
        compiler_params=_cparams("parallel", "arbitrary"),
        name="qkv_prep",
    )(p3, conv_w)


def _delta_kernel(q_ref, k_ref, v_ref, lg_ref, z_ref, alog_ref, dt_ref, dn_ref, o_ref,
                  state_ref, gcb_ref, bb_ref, gct_ref, bt_ref, u_ref, wq_ref, ak_ref, dec_ref, os_ref,
                  *, t, n_heads):
    c = DN_CHUNK
    nc = t // c
    heads = range(n_heads)

    @pl.when(pl.program_id(1) == 0)
    def _():
        state_ref[...] = jnp.zeros_like(state_ref)

    lg = lg_ref[0]
    beta_all = jax.nn.sigmoid(lg)
    sp_in = lg + dt_ref[...]
    softplus = jnp.maximum(sp_in, 0.0) + jnp.log1p(jnp.exp(-jnp.abs(sp_in)))
    g_all = -jnp.exp(alog_ref[...]) * softplus
    ri = lax.broadcasted_iota(jnp.int32, (c, c), 0)
    ci = lax.broadcasted_iota(jnp.int32, (c, c), 1)
    eye = (ri == ci).astype(F32)
    ltri16 = (ri >= ci).astype(BF16)
    gc_chunks = []
    for ic in range(nc):
        g_c = g_all[ic * c:(ic + 1) * c, :]
        g_hi = g_c.astype(BF16)
        r1 = g_c - g_hi.astype(F32)
        g_mid = r1.astype(BF16)
        g_lo = (r1 - g_mid.astype(F32)).astype(BF16)
        parts = _dot(ltri16, jnp.concatenate([g_hi, g_mid, g_lo], axis=1))
        gc_c = parts[:, 0:LANES] + parts[:, LANES:2 * LANES] + parts[:, 2 * LANES:3 * LANES]
        gc_chunks.append(gc_c)
        gct_ref[ic] = gc_c.T
        bt_ref[ic] = beta_all[ic * c:(ic + 1) * c, :].T
    gc_all = jnp.concatenate(gc_chunks, axis=0)
    for h in heads:
        gcb_ref[h] = jnp.broadcast_to(gc_all[:, n_heads + h:n_heads + h + 1], (t, LANES))
        bb_ref[h] = jnp.broadcast_to(beta_all[:, h:h + 1], (t, LANES))

    cpi = next(n for n in (4, 2, 1) if nc % n == 0)

    def intra(it, carry):
        causal = ri >= ci
        blk = ri ^ ci
        units = [(it * cpi + sub, h) for sub in range(cpi) for h in heads]
        nu = range(len(units))
        a_mats, t_mats = [], []
        for ic, h in units:
            rows = pl.ds(pl.multiple_of(ic * c, c), c)
            cols = slice(h * LANES, (h + 1) * LANES)
            k16 = k_ref[0, rows, cols].astype(BF16)
            q16 = q_ref[0, rows, cols].astype(BF16)
            qkk = _dot_nt(jnp.concatenate([q16, k16], axis=0), k16)
            gcr = jnp.broadcast_to(gct_ref[ic, n_heads + h:n_heads + h + 1, :], (c, c))
            gamma = jnp.exp(jnp.where(causal, gcb_ref[h, rows, :] - gcr, -1e30))
            ak_ref[h, ic, 0:c, :] = (qkk[0:c] * gamma).astype(BF16)
            a_mat = jnp.where(ri > ci, bb_ref[h, rows, :] * qkk[c:2 * c] * gamma, 0.0)
            a_mats.append(a_mat)
            t_mats.append(eye - jnp.where(blk == 1, a_mat, 0.0))
        s = 2
        while s < c:
            join = (blk >= s) & (blk < 2 * s)
            t16 = [tm.astype(BF16) for tm in t_mats]
            ys = [_dot(t16[n], jnp.where(join, a_mats[n], 0.0).astype(BF16)) for n in nu]
            t_mats = [t_mats[n] - _dot(ys[n].astype(BF16), t16[n]) for n in nu]
            s *= 2
        for n, (ic, h) in enumerate(units):
            rows = pl.ds(pl.multiple_of(ic * c, c), c)
            cols = slice(h * LANES, (h + 1) * LANES)
            gcb = gcb_ref[h, rows, :]
            kh = k_ref[0, rows, cols]
            e_col = jnp.exp(gcb)
            br = jnp.broadcast_to(bt_ref[ic, h:h + 1, :], (c, c))
            rhs = jnp.concatenate([v_ref[0, rows, cols].astype(BF16), (kh * e_col).astype(BF16)], axis=1)
            uw = _dot((t_mats[n] * br).astype(BF16), rhs)
            u_ref[rows, cols] = uw[:, 0:LANES]
            wq_ref[h, ic, 0:c, :] = uw[:, LANES:2 * LANES].astype(BF16)
            wq_ref[h, ic, c:2 * c, :] = (q_ref[0, rows, cols] * e_col).astype(BF16)
            g_last = gcb[c - 1:c, :]
            ak_ref[h, ic, c:2 * c, :] = (kh * jnp.exp(g_last - gcb)).T.astype(BF16)
            dec_ref[ic, h:h + 1, :] = jnp.exp(g_last)
        return carry

    lax.fori_loop(0, nc // cpi, intra, 0)

    def inter(ic, carry):
        rows = pl.ds(pl.multiple_of(ic * c, c), c)
        sts, qss, vns = [], [], []
        for h in heads:
            cols = slice(h * LANES, (h + 1) * LANES)
            st = state_ref[h]
            wq = _dot(wq_ref[h, ic], st.astype(BF16))
            sts.append(st)
            vns.append((u_ref[rows, cols] - wq[0:c]).astype(BF16))
            qss.append(wq[c:2 * c])
        for h in heads:
            cols = slice(h * LANES, (h + 1) * LANES)
            ak = _dot(ak_ref[h, ic], vns[h])
            os_ref[rows, cols] = qss[h] + ak[0:c]
            state_ref[h] = sts[h] * dec_ref[ic, h:h + 1, :] + ak[c:2 * c]
        return carry

    lax.fori_loop(0, nc, inter, 0)

    for h in heads:
        cols = slice(h * LANES, (h + 1) * LANES)
        o_ref[0, :, cols] = (_rms(os_ref[:, cols], dn_ref[...]) * _silu(z_ref[0, :, cols])).astype(BF16)


def delta_rule(q, k, v, p3, alog_pad, dt_pad, dn_norm, *, n_heads, t, z_blk, lg_blk):
    b, s, width = q.shape
    c = DN_CHUNK
    nc = t // c
    assert width == n_heads * LANES and c == LANES and t % c == 0
    qspec = pl.BlockSpec((1, t, width), lambda bi, ti: (bi, ti, 0))
    row = pl.BlockSpec((1, LANES), lambda bi, ti: (0, 0))
    return pl.pallas_call(
        functools.partial(_delta_kernel, t=t, n_heads=n_heads),
        out_shape=jax.ShapeDtypeStruct((b, s, width), BF16),
        grid=(b, s // t),
        in_specs=[qspec, qspec, qspec,
                  pl.BlockSpec((1, t, LANES), lambda bi, ti: (bi, ti, lg_blk)),
                  pl.BlockSpec((1, t, width), lambda bi, ti: (bi, ti, z_blk)),
                  row, row, row],
        out_specs=qspec,
        scratch_shapes=[pltpu.VMEM((n_heads, LANES, LANES), F32),
                        pltpu.VMEM((n_heads, t, LANES), F32),
                        pltpu.VMEM((n_heads, t, LANES), F32),
                        pltpu.VMEM((nc, LANES, c), F32),
                        pltpu.VMEM((nc, LANES, c), F32),
                        pltpu.VMEM((t, width), F32),
                        pltpu.VMEM((n_heads, nc, 2 * c, LANES), BF16),
                        pltpu.VMEM((n_heads, nc, 2 * c, LANES), BF16),
                        pltpu.VMEM((nc, n_heads, LANES), F32),
                        pltpu.VMEM((t, width), F32)],
        compiler_params=_cparams("parallel", "arbitrary"),
        name="delta_rule",
    )(q, k, v, p3, p3, alog_pad, dt_pad, dn_norm)


def _pool_kernel(xp_ref, w_ref, sc_ref, o_ref, buf_ref, y_ref, *, t, halo, gdim):
    ti = pl.program_id(1)
    per = gdim // LANES
    half = t // 2

    @pl.when(ti == 0)
    def _():
        buf_ref[:, 0:halo, :] = jnp.zeros((buf_ref.shape[0], halo, LANES), F32)

    seen_even = ti * t + 2 * lax.broadcasted_iota(jnp.int32, (half, 1), 0) + 1
    for gi, win in enumerate(POOL_WINDOWS):
        for sl in range(per):
            cg = gi * per + sl
            buf_ref[cg, halo:halo + t, :] = xp_ref[0, :, cg * LANES:(cg + 1) * LANES]
            for par in range(2):
                tok = buf_ref[cg, pl.ds(halo + par, half, stride=2), :]
                acc = tok
                for d in range(1, win):
                    acc = acc + buf_ref[cg, pl.ds(halo + par - d, half, stride=2), :]
                count = jnp.minimum(seen_even + par, win).astype(F32)
                y_ref[cg, pl.ds(par, half, stride=2), :] = acc / count - tok
            buf_ref[cg, 0:halo, :] = buf_ref[cg, t:t + halo, :]
        cols = slice(gi * gdim, (gi + 1) * gdim)
        pooled = jnp.concatenate([y_ref[gi * per + sl] for sl in range(per)], axis=1)
        y = _dot(pooled.astype(BF16), w_ref[gi])
        o_ref[0, :, cols] = (y * sc_ref[:, cols]).astype(BF16)


def pool_mixer(p3, w_grp, scale, *, t, xp_blk):
    b, s, _ = p3.shape
    groups, gdim, _ = w_grp.shape
    width = groups * gdim
    halo = 16
    return pl.pallas_call(
        functools.partial(_pool_kernel, t=t, halo=halo, gdim=gdim),
        out_shape=jax.ShapeDtypeStruct((b, s, width), BF16),
        grid=(b, s // t),
        in_specs=[pl.BlockSpec((1, t, width), lambda bi, ti: (bi, ti, xp_blk)),
                  pl.BlockSpec((groups, gdim, gdim), lambda bi, ti: (0, 0, 0)),
                  pl.BlockSpec((1, width), lambda bi, ti: (0, 0))],
        out_specs=pl.BlockSpec((1, t, width), lambda bi, ti: (bi, ti, 0)),
        scratch_shapes=[pltpu.VMEM((width // LANES, t + halo, LANES), F32),
                        pltpu.VMEM((width // LANES, t, LANES), F32)],
        compiler_params=_cparams("parallel", "arbitrary"),
        name="pool_mixer",
    )(p3, w_grp, scale)


def _conf_conv_kernel(u_ref, w_ref, b_ref, g_ref, beta_ref, o_ref, buf_ref, y_ref, *, t, halo, kw):
    width = u_ref.shape[2]
    ncg = width // LANES
    half = t // 2

    @pl.when(pl.program_id(1) == 0)
    def _():
        buf_ref[:, 0:halo, :] = jnp.zeros((ncg, halo, LANES), F32)

    rsum = jnp.zeros((t, LANES), F32)
    for cg in range(ncg):
        cols = slice(cg * LANES, (cg + 1) * LANES)
        buf_ref[cg, halo:halo + t, :] = u_ref[0, :, cols]
        bias = jnp.broadcast_to(b_ref[:, cols], (half, LANES))
        acc_e, acc_o = bias, bias
        for j in range(kw):
            r0 = halo - (kw - 1) + j
            wj = w_ref[j:j + 1, cols]
            acc_e = acc_e + wj * buf_ref[cg, pl.ds(r0, half, stride=2), :]
            acc_o = acc_o + wj * buf_ref[cg, pl.ds(r0 + 1, half, stride=2), :]
        y_ref[cg, pl.ds(0, half, stride=2), :] = acc_e
        y_ref[cg, pl.ds(1, half, stride=2), :] = acc_o
        buf_ref[cg, 0:halo, :] = buf_ref[cg, t:t + halo, :]
        rsum = rsum + y_ref[cg]
    mu = jnp.sum(rsum, axis=-1, keepdims=True) * (1.0 / width)
    vsum = jnp.zeros((t, LANES), F32)
    for cg in range(ncg):
        yc = y_ref[cg] - mu
        vsum = vsum + yc * yc
    rstd = lax.rsqrt(jnp.sum(vsum, axis=-1, keepdims=True) * (1.0 / width) + EPS)
    for cg in range(ncg):
        cols = slice(cg * LANES, (cg + 1) * LANES)
        yn = (y_ref[cg] - mu) * rstd * g_ref[:, cols] + beta_ref[:, cols]
        o_ref[0, :, cols] = _silu(yn).astype(BF16)


def conf_conv(u3, dw, dw_b, ln_g, ln_b, *, t):
    b, s, width = u3.shape
    kw = dw.shape[0]
    halo = 32
    assert kw - 1 <= halo and t % 16 == 0
    row = pl.BlockSpec((1, width), lambda bi, ti: (0, 0))
    blk = pl.BlockSpec((1, t, width), lambda bi, ti: (bi, ti, 0))
    return pl.pallas_call(
        functools.partial(_conf_conv_kernel, t=t, halo=halo, kw=kw),
        out_shape=jax.ShapeDtypeStruct((b, s, width), BF16),
        grid=(b, s // t),
        in_specs=[blk, pl.BlockSpec((kw, width), lambda bi, ti: (0, 0)), row, row, row],
        out_specs=blk,
        scratch_shapes=[pltpu.VMEM((width // LANES, t + halo, LANES), F32),
                        pltpu.VMEM((width // LANES, t, LANES), F32)],
        compiler_params=_cparams("parallel", "arbitrary"),
        name="conf_conv",
    )(u3, dw, dw_b, ln_g, ln_b)


def _even_layer(x2, b, s, g_pre, g_post, w_in, conv_w, a_log, dt_bias, dn_norm, pool_w, pool_scale, w_out):
    d = x2.shape[1]
    n_heads = a_log.shape[0]
    dn_w = n_heads * dn_norm.shape[0]
    pool_width = pool_scale.shape[0]
    o2 = 4 * dn_w
    o4 = o2 + 2 * n_heads
    tn = TN_EVEN_PROJ
    n_pad = tn * pl.cdiv(4 * dn_w + pool_width + LANES, tn)
    w_cat = jnp.concatenate(
        [w_in[:, :o2], w_in[:, o4:], w_in[:, o2:o4],
         jnp.zeros((d, n_pad - w_in.shape[1]), w_in.dtype)], axis=1).astype(BF16)
    p = norm_proj(x2, g_pre.reshape(1, d), w_cat, tm=TM_PROJ, tn=tn)
    p3 = p.reshape(b, s, n_pad)
    q, k, v = qkv_prep(p3, conv_w, n_heads=n_heads, t=T_QKV)
    pad = jnp.zeros((n_heads,), F32)
    tail = jnp.zeros((LANES - 2 * n_heads,), F32)
    alog_pad = jnp.concatenate([pad, a_log, tail]).reshape(1, LANES)
    dt_pad = jnp.concatenate([pad, dt_bias, tail]).reshape(1, LANES)
    o = delta_rule(q, k, v, p3, alog_pad, dt_pad, dn_norm.reshape(1, LANES), n_heads=n_heads, t=T_DELTA,
                   z_blk=3, lg_blk=(4 * dn_w + pool_width) // LANES)
    y_pool = pool_mixer(p3, pool_w.astype(BF16), pool_scale.reshape(1, pool_width), t=T_POOL,
                        xp_blk=4 * dn_w // pool_width)
    w_out16 = w_out.astype(BF16)
    return out_proj([o.reshape(b * s, dn_w), y_pool.reshape(b * s, pool_width)],
                    [w_out16[:dn_w], w_out16[dn_w:]], x2, g_post.reshape(1, d), tm=TM_OUT)


def _odd_layer(x2, b, s, g_pre, g_post, w_in, dw, dw_b, ln_g, ln_b, w_out):
    d = x2.shape[1]
    width = w_out.shape[0]
    u = norm_glu(x2, g_pre.reshape(1, d), w_in.astype(BF16), tm=TM_PROJ, tn=TN_GLU)
    a = conf_conv(u.reshape(b, s, width), dw, dw_b.reshape(1, width), ln_g.reshape(1, width),
                  ln_b.reshape(1, width), t=T_CONV)
    return out_proj([a.reshape(b * s, width)], [w_out.astype(BF16)], x2, g_post.reshape(1, d), tm=TM_OUT)


def kernel(x, norm_mix_pre, norm_mix_post, norm_mlp_pre, norm_mlp_post, even_w_in, even_conv, even_a_log, even_dt_bias, even_dn_norm, even_pool_w, even_pool_scale, even_w_out, odd_w_in, odd_dw, odd_dw_b, odd_ln_g, odd_ln_b, odd_w_out, mlp_w_up, mlp_w_down):
    b, s, d = x.shape
    depth = norm_mix_pre.shape[0]
    x2 = x.reshape(b * s, d)
    w_up16 = mlp_w_up.astype(BF16)
    w_down16 = mlp_w_down.astype(BF16)
    for i in range(depth):
        j = i // 2
        if i % 2 == 0:
            x2 = _even_layer(x2, b, s, norm_mix_pre[i], norm_mix_post[i], even_w_in[j], even_conv[j],
                             even_a_log[j], even_dt_bias[j], even_dn_norm[j], even_pool_w[j],
                             even_pool_scale[j], even_w_out[j])
        else:
            x2 = _odd_layer(x2, b, s, norm_mix_pre[i], norm_mix_post[i], odd_w_in[j], odd_dw[j],
                            odd_dw_b[j], odd_ln_g[j], odd_ln_b[j], odd_w_out[j])
        x2 = mlp(x2, norm_mlp_pre[i].reshape(1, d), w_up16, w_down16, norm_mlp_post[i].reshape(1, d),
                 layer=i, tm=TM_MLP, tf=TF_MLP)
    return x2.reshape(b, s, d)
```
